```python
import jax, jax.numpy as jnp
from jax import lax
import numpy as np

D_MODEL = 1024
BATCH = 16
SEQ = 2048
DEPTH = 1

PLE_DIM = 256
RET_HEADS = 8
RET_HEAD_DIM = 64
RET_WIDTH = RET_HEADS * RET_HEAD_DIM
RET_CHUNK = 128
MLA_HEADS = 8
MLA_NOPE_DIM = 64
MLA_ROPE_DIM = 32
MLA_QK_DIM = MLA_NOPE_DIM + MLA_ROPE_DIM
MLA_V_DIM = 64
MLA_Q_RANK = 256
MLA_KV_RANK = 128
MLA_WIDTH = MLA_HEADS * MLA_V_DIM
MIX_WIDTH = RET_WIDTH + MLA_WIDTH
Q_BLOCK = 128
IN_COLS = 4 * RET_WIDTH + MLA_Q_RANK + MLA_KV_RANK + MLA_ROPE_DIM
D_FF = 2816
CONV_WIDTH = 3
ROPE_BASE = 10000.0
EPS = 1e-6

kernel_name = "hybrid_retention_mla_convffn_block"


def rms_norm(x, g):
    xf = x.astype(jnp.float32)
    y = xf * lax.rsqrt(jnp.mean(xf * xf, axis=-1, keepdims=True) + EPS)
    return (y * g.astype(jnp.float32)).astype(x.dtype)


def rope(x, pos):
    d = x.shape[-1]
    inv_freq = ROPE_BASE ** (-jnp.arange(0, d, 2, dtype=jnp.float32) / d)
    ang = pos.astype(jnp.float32)[..., None] * inv_freq
    cos = jnp.cos(ang)[:, :, None, :]
    sin = jnp.sin(ang)[:, :, None, :]
    xf = x.astype(jnp.float32)
    x1, x2 = xf[..., : d // 2], xf[..., d // 2:]
    out = jnp.concatenate([x1 * cos - x2 * sin, x2 * cos + x1 * sin], axis=-1)
    return out.astype(x.dtype)


def retention_chunkwise(q, k, v):
    B, S, H, D = q.shape
    C = RET_CHUNK
    N = S // C
    log_gamma = jnp.log1p(-jnp.exp2(-5.0 - jnp.arange(H, dtype=jnp.float32)))
    idx = jnp.arange(C, dtype=jnp.float32)
    rel = idx[:, None] - idx[None, :]
    intra = jnp.where(rel >= 0, jnp.exp(log_gamma[:, None, None] * jnp.maximum(rel, 0.0)), 0.0)
    inner = jnp.exp(log_gamma[:, None] * (idx + 1.0))
    tail = jnp.exp(log_gamma[:, None] * (C - 1.0 - idx))
    chunk_decay = jnp.exp(log_gamma * C)

    def to_chunks(t):
        return t.astype(jnp.float32).reshape(B, N, C, H, D).transpose(1, 0, 3, 2, 4)

    qc = to_chunks(q)
    kc = to_chunks(k) * (D ** -0.5)
    vc = to_chunks(v)

    def step(state, inp):
        q_i, k_i, v_i = inp
        cross = jnp.einsum('bhcd,bhde->bhce', q_i, state) * inner[None, :, :, None]
        scores = jnp.einsum('bhcd,bhmd->bhcm', q_i, k_i) * intra[None]
        within = jnp.einsum('bhcm,bhme->bhce', scores, v_i)
        state = state * chunk_decay[None, :, None, None] + jnp.einsum(
            'bhmd,bhme->bhde', k_i * tail[None, :, :, None], v_i)
        return state, cross + within

    state0 = jnp.zeros((B, H, D, D), jnp.float32)
    _, out = lax.scan(step, state0, (qc, kc, vc))
    return out.transpose(1, 0, 3, 2, 4).reshape(B, S, H, D)


def causal_block_attention(q, k, v):
    B, S, H, Dq = q.shape
    Dv = v.shape[-1]
    NB = S // Q_BLOCK
    scale = Dq ** -0.5
    qb = q.reshape(B, NB, Q_BLOCK, H, Dq).transpose(1, 0, 2, 3, 4)
    key_idx = jnp.arange(S)

    def one_block(args):
        q_blk, blk = args
        s = jnp.einsum('bqhd,bkhd->bhqk', q_blk, k).astype(jnp.float32) * scale
        q_idx = blk * Q_BLOCK + jnp.arange(Q_BLOCK)
        mask = key_idx[None, :] <= q_idx[:, None]
        s = jnp.where(mask[None, None], s, -jnp.inf)
        pr = jax.nn.softmax(s, axis=-1).astype(v.dtype)
        return jnp.einsum('bhqk,bkhe->bqhe', pr, v)

    out = lax.map(one_block, (qb, jnp.arange(NB)))
    return out.transpose(1, 0, 2, 3, 4).reshape(B, S, H * Dv)


def causal_depthwise_conv(h, w, b):
    C = h.shape[-1]
    y = lax.conv_general_dilated(
        h, w[:, None, :].astype(h.dtype), window_strides=(1,),
        padding=[(CONV_WIDTH - 1, 0)], dimension_numbers=('NWC', 'WIO', 'NWC'),
        feature_group_count=C)
    return y + b.astype(h.dtype)


def setup_inputs(seed: int = 0) -> dict:
    key = jax.random.key(seed)
    ks = jax.random.split(key, 24)
    f32 = jnp.float32

    def w(k, shape, fan_in):
        return jax.random.normal(k, shape, f32) * (fan_in ** -0.5)

    def gain(k, shape):
        return 1.0 + 0.05 * jax.random.normal(k, shape, f32)

    x = jax.random.normal(ks[0], (BATCH, SEQ, D_MODEL), f32)
    p = jax.random.normal(ks[1], (DEPTH, BATCH, SEQ, PLE_DIM), f32)
    offsets = jax.random.randint(ks[2], (BATCH, 1), 0, 4096, dtype=jnp.int32)
    positions = jnp.arange(SEQ, dtype=jnp.int32)[None, :] + offsets
    return {
        "x": x,
        "p": p,
        "positions": positions,
        "w_in": w(ks[3], (DEPTH, D_MODEL, IN_COLS), D_MODEL),
        "w_uq": w(ks[4], (DEPTH, MLA_Q_RANK, MLA_HEADS * MLA_QK_DIM), MLA_Q_RANK),
        "w_ukv": w(ks[5], (DEPTH, MLA_KV_RANK, MLA_HEADS * (MLA_NOPE_DIM + MLA_V_DIM)), MLA_KV_RANK),
        "w_o": w(ks[6], (DEPTH, MIX_WIDTH, D_MODEL), MIX_WIDTH),
        "g_pre_mix": gain(ks[7], (DEPTH, D_MODEL)),
        "g_post_mix": gain(ks[8], (DEPTH, D_MODEL)),
        "g_q_a": gain(ks[9], (DEPTH, MLA_Q_RANK)),
        "g_kv_a": gain(ks[10], (DEPTH, MLA_KV_RANK)),
        "g_ret_gn": gain(ks[11], (DEPTH, RET_WIDTH)),
        "w_up": w(ks[12], (DEPTH, D_MODEL, 2 * D_FF), D_MODEL),
        "conv_w": w(ks[13], (DEPTH, CONV_WIDTH, 2 * D_FF), CONV_WIDTH),
        "conv_b": 0.02 * jax.random.normal(ks[14], (DEPTH, 2 * D_FF), f32),
        "w_down": w(ks[15], (DEPTH, D_FF, D_MODEL), D_FF),
        "g_pre_ffn": gain(ks[16], (DEPTH, D_MODEL)),
        "g_post_ffn": gain(ks[17], (DEPTH, D_MODEL)),
        "w_ple": w(ks[18], (DEPTH, PLE_DIM, D_MODEL), PLE_DIM),
        "w_ple_gate": w(ks[19], (DEPTH, D_MODEL, D_MODEL), D_MODEL),
        "b_ple_gate": 0.02 * jax.random.normal(ks[20], (DEPTH, D_MODEL), f32),
        "g_post_ple": gain(ks[21], (DEPTH, D_MODEL)),
    }


def reference(x, p, positions, w_in, w_uq, w_ukv, w_o, g_pre_mix, g_post_mix, g_q_a, g_kv_a,
              g_ret_gn, w_up, conv_w, conv_b, w_down, g_pre_ffn, g_post_ffn, w_ple,
              w_ple_gate, b_ple_gate, g_post_ple):
    B, S, _ = x.shape
    splits = [RET_WIDTH, 2 * RET_WIDTH, 3 * RET_WIDTH, 4 * RET_WIDTH,
              4 * RET_WIDTH + MLA_Q_RANK, 4 * RET_WIDTH + MLA_Q_RANK + MLA_KV_RANK]
    for i in range(DEPTH):
        h = rms_norm(x, g_pre_mix[i])
        z = h @ w_in[i]
        q_r, k_r, v_r, g_r, c_q, c_kv, k_pe = jnp.split(z, splits, axis=-1)

        q_r = rope(q_r.reshape(B, S, RET_HEADS, RET_HEAD_DIM), positions)
        k_r = rope(k_r.reshape(B, S, RET_HEADS, RET_HEAD_DIM), positions)
        v_r = v_r.reshape(B, S, RET_HEADS, RET_HEAD_DIM)
        o_r = retention_chunkwise(q_r, k_r, v_r)
        mu = jnp.mean(o_r, axis=-1, keepdims=True)
        var = jnp.mean(jnp.square(o_r - mu), axis=-1, keepdims=True)
        o_r = ((o_r - mu) * lax.rsqrt(var + EPS)).reshape(B, S, RET_WIDTH)
        o_r = (o_r * g_ret_gn[i].astype(jnp.float32)).astype(x.dtype)
        y_ret = jax.nn.silu(g_r) * o_r

        q_m = (rms_norm(c_q, g_q_a[i]) @ w_uq[i]).reshape(B, S, MLA_HEADS, MLA_QK_DIM)
        q_m = jnp.concatenate([q_m[..., :MLA_NOPE_DIM], rope(q_m[..., MLA_NOPE_DIM:], positions)], axis=-1)
        kv = (rms_norm(c_kv, g_kv_a[i]) @ w_ukv[i]).reshape(B, S, MLA_HEADS, MLA_NOPE_DIM + MLA_V_DIM)
        k_nope, v_m = kv[..., :MLA_NOPE_DIM], kv[..., MLA_NOPE_DIM:]
        k_rot = rope(k_pe[:, :, None, :], positions)
        k_m = jnp.concatenate(
            [k_nope, jnp.broadcast_to(k_rot, (B, S, MLA_HEADS, MLA_ROPE_DIM))], axis=-1)
        y_mla = causal_block_attention(q_m, k_m, v_m)

        mix = jnp.concatenate([y_ret, y_mla], axis=-1) @ w_o[i]
        x = x + rms_norm(mix, g_post_mix[i])

        h = rms_norm(x, g_pre_ffn[i])
        u = causal_depthwise_conv(h @ w_up[i], conv_w[i], conv_b[i])
        gate, up = u[..., :D_FF], u[..., D_FF:]
        f = (jax.nn.gelu(gate, approximate=True) * up) @ w_down[i]
        x = x + rms_norm(f, g_post_ffn[i])

        e = (p[i] @ w_ple[i]) * jax.nn.sigmoid(x @ w_ple_gate[i] + b_ple_gate[i])
        x = x + rms_norm(e, g_post_ple[i])
    return x
```

```python
import functools

import jax
import jax.numpy as jnp
from jax import lax
from jax.experimental import pallas as pl
from jax.experimental.pallas import tpu as pltpu

F32 = jnp.float32
BF16 = jnp.bfloat16

D_MODEL = 1024
PLE_DIM = 256
RET_HEADS = 8
RET_HEAD_DIM = 64
RET_WIDTH = RET_HEADS * RET_HEAD_DIM
RET_CHUNK = 128
MLA_HEADS = 8
MLA_NOPE_DIM = 64
MLA_ROPE_DIM = 32
MLA_QK_DIM = MLA_NOPE_DIM + MLA_ROPE_DIM
MLA_V_DIM = 64
MLA_Q_RANK = 256
MLA_KV_RANK = 128
MLA_WIDTH = MLA_HEADS * MLA_V_DIM
D_FF = 2816
ROPE_BASE = 10000.0
EPS = 1e-6

LANES = 128
SUBLANES = 8
IN_USED = 4 * RET_WIDTH + MLA_Q_RANK + MLA_KV_RANK
IN_PAD = IN_USED + LANES
MLA_PAD = MLA_HEADS * LANES
N_FREQ_RET = RET_HEAD_DIM // 2
N_FREQ_MLA = MLA_ROPE_DIM // 2
FF_CHUNK = 256
N_FF_CHUNKS = D_FF // FF_CHUNK

TM_PROJ = 512
TM_POST = 512
TQ_MLA = 256
VMEM_LIMIT = 56 * 1024 * 1024


def _dot(a, b):
    return jnp.dot(a, b, preferred_element_type=F32)


def _dot_nt(a, b):
    return lax.dot_general(a, b, (((1,), (1,)), ((), ())), preferred_element_type=F32)


def _dot_tn(a, b):
    return lax.dot_general(a, b, (((0,), (0,)), ((), ())), preferred_element_type=F32)


def _rms(x, g):
    return x * lax.rsqrt(jnp.mean(x * x, axis=-1, keepdims=True) + EPS) * g


def _const_spec(shape):
    zeros = (0,) * len(shape)
    return pl.BlockSpec(shape, lambda *_: zeros, pipeline_mode=pl.Buffered(1))


def _proj_kernel(pos_ref, x_ref, invf_ref, gpre_ref, win_ref, gqa_ref, wuq_ref, gkva_ref, wukv_ref,
                 qr_ref, kr_ref, vr_ref, gr_ref, qm_ref, km_ref, vm_ref):
    h = _rms(x_ref[...], gpre_ref[...]).astype(BF16)

    ang = invf_ref[...] * pos_ref[0].astype(F32)
    cos_t = jnp.cos(ang)
    sin_t = jnp.sin(ang)
    reps_r = LANES // N_FREQ_RET
    reps_m = LANES // N_FREQ_MLA
    cos_r = jnp.concatenate([cos_t[:N_FREQ_RET]] * reps_r, axis=0).T
    sin_r = jnp.concatenate([sin_t[:N_FREQ_RET]] * reps_r, axis=0).T
    cos_m = jnp.concatenate([cos_t[N_FREQ_RET:]] * reps_m, axis=0).T
    sin_m = jnp.concatenate([sin_t[N_FREQ_RET:]] * reps_m, axis=0).T

    lane = lax.broadcasted_iota(jnp.int32, (1, LANES), 1)
    first_half = (lane & (RET_HEAD_DIM - 1)) < N_FREQ_RET
    sin_r_lo = jnp.where(first_half, -sin_r, 0.0)
    sin_r_hi = jnp.where(first_half, 0.0, sin_r)

    def rope_ret(blk):
        return (blk * cos_r + pltpu.roll(blk, LANES - N_FREQ_RET, 1) * sin_r_lo
                + pltpu.roll(blk, N_FREQ_RET, 1) * sin_r_hi)

    r0 = MLA_NOPE_DIM
    r1 = MLA_NOPE_DIM + N_FREQ_MLA
    r2 = MLA_NOPE_DIM + MLA_ROPE_DIM
    cos_m_f = jnp.where(lane < r0, 1.0, jnp.where(lane < r2, cos_m, 0.0))
    sin_m_lo = jnp.where((lane >= r0) & (lane < r1), -sin_m, 0.0)
    sin_m_hi = jnp.where((lane >= r1) & (lane < r2), sin_m, 0.0)

    def rope_mla(blk):
        return (blk * cos_m_f + pltpu.roll(blk, LANES - N_FREQ_MLA, 1) * sin_m_lo
                + pltpu.roll(blk, N_FREQ_MLA, 1) * sin_m_hi)

    n_blk = RET_WIDTH // LANES
    zq = _dot(h, win_ref[:, 0:RET_WIDTH])
    for c in range(n_blk):
        sl = slice(c * LANES, (c + 1) * LANES)
        qr_ref[:, sl] = rope_ret(zq[:, sl]).astype(BF16)
    zk = _dot(h, win_ref[:, RET_WIDTH:2 * RET_WIDTH])
    k_scale = RET_HEAD_DIM ** -0.5
    for c in range(n_blk):
        sl = slice(c * LANES, (c + 1) * LANES)
        kr_ref[:, sl] = (rope_ret(zk[:, sl]) * k_scale).astype(BF16)
    vr_ref[...] = _dot(h, win_ref[:, 2 * RET_WIDTH:3 * RET_WIDTH]).astype(BF16)
    zg = _dot(h, win_ref[:, 3 * RET_WIDTH:4 * RET_WIDTH])
    gr_ref[...] = jax.nn.silu(zg).astype(BF16)

    zc = _dot(h, win_ref[:, 4 * RET_WIDTH:IN_PAD])
    cq = _rms(zc[:, :MLA_Q_RANK], gqa_ref[...]).astype(BF16)
    ckv = _rms(zc[:, MLA_Q_RANK:MLA_Q_RANK + MLA_KV_RANK], gkva_ref[...]).astype(BF16)
    k_rot = rope_mla(zc[:, MLA_Q_RANK + MLA_KV_RANK:])

    q_scale = MLA_QK_DIM ** -0.5
    qm = _dot(cq, wuq_ref[...])
    for hh in range(MLA_HEADS):
        sl = slice(hh * LANES, (hh + 1) * LANES)
        qm_ref[:, sl] = (rope_mla(qm[:, sl]) * q_scale).astype(BF16)
    kv = _dot(ckv, wukv_ref[...])
    for hh in range(MLA_HEADS):
        sl = slice(hh * LANES, (hh + 1) * LANES)
        km_ref[:, sl] = (kv[:, sl] + k_rot).astype(BF16)
    vm_ref[...] = kv[:, MLA_PAD:].astype(BF16)


def _proj(pos3, x2d, invf, g_pre, w_in_p, g_qa, w_uq_p, g_kva, w_ukv_p):
    t = x2d.shape[0]
    tm = TM_PROJ
    row = lambda i: (i, 0)
    out_widths = (RET_WIDTH, RET_WIDTH, RET_WIDTH, RET_WIDTH, MLA_PAD, MLA_PAD, MLA_WIDTH)
    return pl.pallas_call(
        _proj_kernel,
        grid=(t // tm,),
        in_specs=[
            pl.BlockSpec((1, 1, tm), lambda i: (i, 0, 0)),
            pl.BlockSpec((tm, D_MODEL), row),
            _const_spec(invf.shape),
            _const_spec(g_pre.shape),
            _const_spec(w_in_p.shape),
            _const_spec(g_qa.shape),
            _const_spec(w_uq_p.shape),
            _const_spec(g_kva.shape),
            _const_spec(w_ukv_p.shape),
        ],
        out_specs=[pl.BlockSpec((tm, w), row) for w in out_widths],
        out_shape=[jax.ShapeDtypeStruct((t, w), BF16) for w in out_widths],
        compiler_params=pltpu.CompilerParams(
            dimension_semantics=("parallel",), vmem_limit_bytes=VMEM_LIMIT),
        name="proj",
    )(pos3, x2d, invf, g_pre, w_in_p, g_qa, w_uq_p, g_kva, w_ukv_p)


def _ret_kernel(lg_ref, q_ref, k_ref, v_ref, g_ref, gn_ref, o_ref):
    hp = pl.program_id(1)
    c = RET_CHUNK
    d = RET_HEAD_DIM
    lg0 = lg_ref[2 * hp]
    lg1 = lg_ref[2 * hp + 1]
    lane = lax.broadcasted_iota(jnp.int32, (1, LANES), 1)
    head0 = lane < d
    lgl = jnp.where(head0, lg0, lg1)
    ri = lax.broadcasted_iota(jnp.int32, (c, c), 0)
    ci = lax.broadcasted_iota(jnp.int32, (c, c), 1)
    rel = (ri - ci).astype(F32)
    causal = rel >= 0
    relp = jnp.maximum(rel, 0.0)
    intra0 = jnp.where(causal, jnp.exp(lg0 * relp), 0.0)
    intra1 = jnp.where(causal, jnp.exp(lg1 * relp), 0.0)
    rowf = lax.broadcasted_iota(jnp.int32, (c, 1), 0).astype(F32)
    inner = jnp.exp(lgl * (rowf + 1.0))
    tail = jnp.exp(lgl * (c - 1.0 - rowf))
    decay = jnp.exp(lgl * float(c))
    same_head = (ri < d) == (ci < d)
    gn = gn_ref[...]
    inv_d = 1.0 / d

    def step(i, state):
        sl = pl.ds(pl.multiple_of(i * c, c), c)
        q = q_ref[sl, :]
        k = k_ref[sl, :]
        v = v_ref[sl, :]
        cross = _dot(q, state.astype(BF16)) * inner
        zero = jnp.zeros_like(q)
        s0 = _dot_nt(jnp.where(head0, q, zero), k) * intra0
        s1 = _dot_nt(jnp.where(head0, zero, q), k) * intra1
        w0 = _dot(s0.astype(BF16), v)
        w1 = _dot(s1.astype(BF16), v)
        o = cross + jnp.where(head0, w0, w1)
        kt = (k.astype(F32) * tail).astype(BF16)
        upd = _dot_tn(kt, v)
        new_state = state * decay + jnp.where(same_head, upd, 0.0)

        mu0 = jnp.sum(jnp.where(head0, o, 0.0), axis=-1, keepdims=True)
        mu1 = jnp.sum(jnp.where(head0, 0.0, o), axis=-1, keepdims=True)
        dlt = o - jnp.where(head0, mu0, mu1) * inv_d
        sq = dlt * dlt
        v0 = jnp.sum(jnp.where(head0, sq, 0.0), axis=-1, keepdims=True)
        v1 = jnp.sum(jnp.where(head0, 0.0, sq), axis=-1, keepdims=True)
        var = jnp.where(head0, v0, v1) * inv_d
        on = dlt * lax.rsqrt(var + EPS) * gn
        o_ref[sl, :] = (g_ref[sl, :].astype(F32) * on).astype(BF16)
        return new_state

    n = q_ref.shape[0] // c
    lax.fori_loop(0, n, step, jnp.zeros((LANES, LANES), F32))


def _retention(log_gamma, q_r, k_r, v_r, g_r, g_gn, batch, seq):
    t = q_r.shape[0]
    blk = pl.BlockSpec((seq, LANES), lambda b, hp: (b, hp))
    return pl.pallas_call(
        _ret_kernel,
        grid=(batch, RET_WIDTH // LANES),
        in_specs=[
            pl.BlockSpec(memory_space=pltpu.SMEM),
            blk, blk, blk, blk,
            pl.BlockSpec((1, LANES), lambda b, hp: (0, hp)),
        ],
        out_specs=blk,
        out_shape=jax.ShapeDtypeStruct((t, RET_WIDTH), BF16),
        compiler_params=pltpu.CompilerParams(
            dimension_semantics=("parallel", "parallel"), vmem_limit_bytes=VMEM_LIMIT),
        name="retention",
    )(log_gamma, q_r, k_r, v_r, g_r, g_gn)


def _mla_kernel(q_ref, k_ref, v_ref, o_ref):
    qi = pl.program_id(2)
    tq = q_ref.shape[0]
    tk = tq
    lane = lax.broadcasted_iota(jnp.int32, (1, LANES), 1)
    head0 = lane < MLA_V_DIM
    ri = lax.broadcasted_iota(jnp.int32, (tq, tk), 0)
    ci = lax.broadcasted_iota(jnp.int32, (tq, tk), 1)
    diag_ok = ci <= ri

    def one_head(hh):
        hsl = slice(hh * LANES, (hh + 1) * LANES)
        q = q_ref[:, hsl]

        def update(carry, s, v):
            m, l, acc = carry
            m_new = jnp.maximum(m, jnp.max(s, axis=-1, keepdims=True))
            alpha = jnp.exp(m - m_new)
            p = jnp.exp(s - m_new)
            l_new = alpha * l + jnp.sum(p, axis=-1, keepdims=True)
            acc_new = alpha * acc + _dot(p.astype(BF16), v)
            return m_new, l_new, acc_new

        def body(j, carry):
            sl = pl.ds(pl.multiple_of(j * tk, tk), tk)
            s = _dot_nt(q, k_ref[sl, hsl])
            return update(carry, s, v_ref[sl, :])

        init = (jnp.full((tq, 1), -jnp.inf, F32), jnp.zeros((tq, 1), F32),
                jnp.zeros((tq, LANES), F32))
        carry = lax.fori_loop(0, qi, body, init)
        sl = pl.ds(pl.multiple_of(qi * tk, tk), tk)
        s = jnp.where(diag_ok, _dot_nt(q, k_ref[sl, hsl]), -jnp.inf)
        _, l, acc = update(carry, s, v_ref[sl, :])
        return acc / l

    o_ref[...] = jnp.where(head0, one_head(0), one_head(1)).astype(BF16)


def _mla(q_m, k_m, v_m, batch, seq):
    t = q_m.shape[0]
    tq = TQ_MLA
    nq = seq // tq
    return pl.pallas_call(
        _mla_kernel,
        grid=(batch, MLA_WIDTH // LANES, nq),
        in_specs=[
            pl.BlockSpec((tq, 2 * LANES), lambda b, hp, qi: (b * nq + qi, hp)),
            pl.BlockSpec((seq, 2 * LANES), lambda b, hp, qi: (b, hp)),
            pl.BlockSpec((seq, LANES), lambda b, hp, qi: (b, hp)),
        ],
        out_specs=pl.BlockSpec((tq, LANES), lambda b, hp, qi: (b * nq + qi, hp)),
        out_shape=jax.ShapeDtypeStruct((t, MLA_WIDTH), BF16),
        compiler_params=pltpu.CompilerParams(
            dimension_semantics=("parallel", "parallel", "arbitrary"), vmem_limit_bytes=VMEM_LIMIT),
        name="mla_attention",
    )(q_m, k_m, v_m)


def _causal_conv(u, cw, halo):
    row = lax.broadcasted_iota(jnp.int32, halo.shape, 0)

    def shifted(n):
        r = pltpu.roll(u, n, 0)
        first = jnp.where(row < n, pltpu.roll(halo, n, 0), r[:SUBLANES])
        return jnp.concatenate([first, r[SUBLANES:]], axis=0)

    return cw[0:1] * shifted(2) + cw[1:2] * shifted(1) + cw[2:3] * u + cw[3:4]


def _post_kernel(x_ref, yr_ref, ym_ref, p_ref, wo_ref, gpm_ref, gpf_ref, wup_ref, cw_ref, wdn_ref,
                 gff_ref, wple_ref, wgate_ref, bgate_ref, gple_ref, out_ref,
                 halo_ref, x1_ref, h2_ref, f_ref):
    tm = x_ref.shape[0]

    @pl.when(pl.program_id(1) == 0)
    def _():
        halo_ref[...] = jnp.zeros_like(halo_ref)

    mix = _dot(yr_ref[...], wo_ref[0:RET_WIDTH, :]) + _dot(ym_ref[...], wo_ref[RET_WIDTH:, :])
    x1 = x_ref[...] + _rms(mix, gpm_ref[...])
    x1_ref[...] = x1
    h2_ref[...] = _rms(x1, gpf_ref[...]).astype(BF16)
    f_ref[...] = jnp.zeros_like(f_ref)

    def chunk(c, carry):
        hb = h2_ref[...]
        ug = _dot(hb, wup_ref[c])
        uu = _dot(hb, wup_ref[c + N_FF_CHUNKS])
        gate = _causal_conv(ug, cw_ref[c], halo_ref[c])
        up = _causal_conv(uu, cw_ref[c + N_FF_CHUNKS], halo_ref[c + N_FF_CHUNKS])
        halo_ref[c] = ug[tm - SUBLANES:, :]
        halo_ref[c + N_FF_CHUNKS] = uu[tm - SUBLANES:, :]
        act = (jax.nn.gelu(gate, approximate=True) * up).astype(BF16)
        f_ref[...] += _dot(act, wdn_ref[c])
        return carry

    lax.fori_loop(0, N_FF_CHUNKS, chunk, 0)

    x2 = x1_ref[...] + _rms(f_ref[...], gff_ref[...])
    gate = jax.nn.sigmoid(_dot(x2.astype(BF16), wgate_ref[...]) + bgate_ref[...])
    e = _dot(p_ref[...].astype(BF16), wple_ref[...]) * gate
    out_ref[...] = x2 + _rms(e, gple_ref[...])


def _post(x2d, y_ret, y_mla, p2d, w_o, g_pm, g_pf, w_up_c, cw_c, w_dn_c, g_ff, w_ple, w_gate, b_gate,
          g_ple, batch, seq):
    t = x2d.shape[0]
    tm = TM_POST
    ns = seq // tm
    row = lambda b, s: (b * ns + s, 0)
    consts = (w_o, g_pm, g_pf, w_up_c, cw_c, w_dn_c, g_ff, w_ple, w_gate, b_gate, g_ple)
    return pl.pallas_call(
        _post_kernel,
        grid=(batch, ns),
        in_specs=[
            pl.BlockSpec((tm, D_MODEL), row),
            pl.BlockSpec((tm, RET_WIDTH), row),
            pl.BlockSpec((tm, MLA_WIDTH), row),
            pl.BlockSpec((tm, PLE_DIM), row),
        ] + [_const_spec(a.shape) for a in consts],
        out_specs=pl.BlockSpec((tm, D_MODEL), row),
        out_shape=jax.ShapeDtypeStruct((t, D_MODEL), F32),
        scratch_shapes=[
            pltpu.VMEM((2 * N_FF_CHUNKS, SUBLANES, FF_CHUNK), F32),
            pltpu.VMEM((tm, D_MODEL), F32),
            pltpu.VMEM((tm, D_MODEL), BF16),
            pltpu.VMEM((tm, D_MODEL), F32),
        ],
        compiler_params=pltpu.CompilerParams(
            dimension_semantics=("arbitrary", "arbitrary"), vmem_limit_bytes=VMEM_LIMIT),
        name="post",
    )(x2d, y_ret, y_mla, p2d, *consts)


def _layer(x2d, p2d, pos3, batch, seq, w_in, w_uq, w_ukv, w_o, g_pre_mix, g_post_mix, g_q_a, g_kv_a,
           g_ret_gn, w_up, conv_w, conv_b, w_down, g_pre_ffn, g_post_ffn, w_ple, w_ple_gate,
           b_ple_gate, g_post_ple):
    row = lambda v: v.reshape(1, -1)

    kpe_blk = jnp.pad(w_in[:, IN_USED:], ((0, 0), (MLA_NOPE_DIM, LANES - MLA_QK_DIM)))
    w_in_p = jnp.concatenate([w_in[:, :IN_USED], kpe_blk], axis=1).astype(BF16)
    w_uq_p = jnp.pad(w_uq.reshape(MLA_Q_RANK, MLA_HEADS, MLA_QK_DIM),
                     ((0, 0), (0, 0), (0, LANES - MLA_QK_DIM))).reshape(MLA_Q_RANK, MLA_PAD).astype(BF16)
    w_ukv3 = w_ukv.reshape(MLA_KV_RANK, MLA_HEADS, MLA_NOPE_DIM + MLA_V_DIM)
    w_uk_p = jnp.pad(w_ukv3[..., :MLA_NOPE_DIM],
                     ((0, 0), (0, 0), (0, LANES - MLA_NOPE_DIM))).reshape(MLA_KV_RANK, MLA_PAD)
    w_uv = w_ukv3[..., MLA_NOPE_DIM:].reshape(MLA_KV_RANK, MLA_WIDTH)
    w_ukv_p = jnp.concatenate([w_uk_p, w_uv], axis=1).astype(BF16)
    w_up_c = w_up.reshape(D_MODEL, 2 * N_FF_CHUNKS, FF_CHUNK).transpose(1, 0, 2).astype(BF16)
    w_dn_c = w_down.reshape(N_FF_CHUNKS, FF_CHUNK, D_MODEL).astype(BF16)
    cw4 = jnp.concatenate([conv_w, conv_b[None, :]], axis=0)
    cw_c = jnp.pad(cw4, ((0, SUBLANES - 4), (0, 0))).reshape(
        SUBLANES, 2 * N_FF_CHUNKS, FF_CHUNK).transpose(1, 0, 2)

    invf_r = ROPE_BASE ** (-jnp.arange(0, RET_HEAD_DIM, 2, dtype=F32) / RET_HEAD_DIM)
    invf_m = ROPE_BASE ** (-jnp.arange(0, MLA_ROPE_DIM, 2, dtype=F32) / MLA_ROPE_DIM)
    invf = jnp.concatenate([invf_r, invf_m]).reshape(-1, 1)
    log_gamma = jnp.log1p(-jnp.exp2(-5.0 - jnp.arange(RET_HEADS, dtype=F32)))

    q_r, k_r, v_r, g_r, q_m, k_m, v_m = _proj(
        pos3, x2d, invf, row(g_pre_mix), w_in_p, row(g_q_a), w_uq_p, row(g_kv_a), w_ukv_p)
    y_ret = _retention(log_gamma, q_r, k_r, v_r, g_r, row(g_ret_gn), batch, seq)
    y_mla = _mla(q_m, k_m, v_m, batch, seq)
    return _post(x2d, y_ret, y_mla, p2d, w_o.astype(BF16), row(g_post_mix), row(g_pre_ffn), w_up_c, cw_c,
                 w_dn_c, row(g_post_ffn), w_ple.astype(BF16), w_ple_gate.astype(BF16), row(b_ple_gate),
                 row(g_post_ple), batch, seq)


def kernel(x, p, positions, w_in, w_uq, w_ukv, w_o, g_pre_mix, g_post_mix, g_q_a, g_kv_a, g_ret_gn,
           w_up, conv_w, conv_b, w_down, g_pre_ffn, g_post_ffn, w_ple, w_ple_gate, b_ple_gate,
           g_post_ple):
    batch, seq, _ = x.shape
    depth = w_in.shape[0]
    assert seq % TM_PROJ == 0 and seq % TM_POST == 0 and seq % TQ_MLA == 0 and seq % RET_CHUNK == 0
    t = batch * seq
    x2d = x.reshape(t, D_MODEL)
    pos3 = positions.reshape(t // TM_PROJ, 1, TM_PROJ)
    for i in range(depth):
        x2d = _layer(x2d, p[i].reshape(t, PLE_DIM), pos3, batch, seq, w_in[i], w_uq[i], w_ukv[i], w_o[i],
                     g_pre_mix[i], g_post_mix[i], g_q_a[i], g_kv_a[i], g_ret_gn[i], w_up[i], conv_w[i],
                     conv_b[i], w_down[i], g_pre_ffn[i], g_post_ffn[i], w_ple[i], w_ple_gate[i],
                     b_ple_gate[i], g_post_ple[i])
    return x2d.reshape(batch, seq, D_MODEL)
```

```python
import functools

import jax
import jax.numpy as jnp
from jax import lax
from jax.experimental import pallas as pl
from jax.experimental.pallas import tpu as pltpu

F32 = jnp.float32
BF16 = jnp.bfloat16

D_MODEL = 1024
PLE_DIM = 256
RET_HEADS = 8
RET_HEAD_DIM = 64
RET_WIDTH = RET_HEADS * RET_HEAD_DIM
RET_CHUNK = 128
MLA_HEADS = 8
MLA_NOPE_DIM = 64
MLA_ROPE_DIM = 32
MLA_QK_DIM = MLA_NOPE_DIM + MLA_ROPE_DIM
MLA_V_DIM = 64
MLA_Q_RANK = 256
MLA_KV_RANK = 128
MLA_WIDTH = MLA_HEADS * MLA_V_DIM
D_FF = 2816
ROPE_BASE = 10000.0
EPS = 1e-6
LOG2_E = 1.4426950408889634

LANES = 128
SUBLANES = 8
IN_USED = 4 * RET_WIDTH + MLA_Q_RANK + MLA_KV_RANK
IN_PAD = IN_USED + LANES
MLA_PAD = MLA_HEADS * LANES
N_FREQ_RET = RET_HEAD_DIM // 2
N_FREQ_MLA = MLA_ROPE_DIM // 2
FF_CHUNK = 256
N_FF_CHUNKS = D_FF // FF_CHUNK

TM_PROJ = 512
TM_POST = 512
TQ_MLA = 256
VMEM_LIMIT = 56 * 1024 * 1024


def _dot(a, b):
    return jnp.dot(a, b, preferred_element_type=F32)


def _dot_nt(a, b):
    return lax.dot_general(a, b, (((1,), (1,)), ((), ())), preferred_element_type=F32)


def _dot_tn(a, b):
    return lax.dot_general(a, b, (((0,), (0,)), ((), ())), preferred_element_type=F32)


def _rms(x, g):
    return x * lax.rsqrt(jnp.mean(x * x, axis=-1, keepdims=True) + EPS) * g


def _const_spec(shape):
    zeros = (0,) * len(shape)
    return pl.BlockSpec(shape, lambda *_: zeros, pipeline_mode=pl.Buffered(1))


def _proj_kernel(pos_ref, x_ref, invf_ref, gpre_ref, win_ref, gqa_ref, wuqt_ref, gkva_ref, wuk_ref, wuvt_ref,
                 qr_ref, kr_ref, vr_ref, gr_ref, km_ref, qt_ref, vt_ref):
    tm = x_ref.shape[0]
    h = _rms(x_ref[...], gpre_ref[...]).astype(BF16)

    ang = invf_ref[...] * pos_ref[0].astype(F32)
    cos_t = jnp.cos(ang)
    sin_t = jnp.sin(ang)
    reps_r = LANES // N_FREQ_RET
    reps_m = LANES // N_FREQ_MLA
    cos_r = jnp.concatenate([cos_t[:N_FREQ_RET]] * reps_r, axis=0).T
    sin_r = jnp.concatenate([sin_t[:N_FREQ_RET]] * reps_r, axis=0).T
    cos_m = jnp.concatenate([cos_t[N_FREQ_RET:]] * reps_m, axis=0).T
    sin_m = jnp.concatenate([sin_t[N_FREQ_RET:]] * reps_m, axis=0).T

    lane = lax.broadcasted_iota(jnp.int32, (1, LANES), 1)
    first_half = (lane & (RET_HEAD_DIM - 1)) < N_FREQ_RET
    sin_r_lo = jnp.where(first_half, -sin_r, 0.0)
    sin_r_hi = jnp.where(first_half, 0.0, sin_r)

    def rope_ret(blk):
        return (blk * cos_r + pltpu.roll(blk, LANES - N_FREQ_RET, 1) * sin_r_lo
                + pltpu.roll(blk, N_FREQ_RET, 1) * sin_r_hi)

    r0 = MLA_NOPE_DIM
    r1 = MLA_NOPE_DIM + N_FREQ_MLA
    r2 = MLA_NOPE_DIM + MLA_ROPE_DIM
    cos_m_f = jnp.where(lane < r0, 1.0, jnp.where(lane < r2, cos_m, 0.0))
    sin_m_lo = jnp.where((lane >= r0) & (lane < r1), -sin_m, 0.0)
    sin_m_hi = jnp.where((lane >= r1) & (lane < r2), sin_m, 0.0)

    def rope_mla(blk):
        return (blk * cos_m_f + pltpu.roll(blk, LANES - N_FREQ_MLA, 1) * sin_m_lo
                + pltpu.roll(blk, N_FREQ_MLA, 1) * sin_m_hi)

    n_blk = RET_WIDTH // LANES
    zq = _dot(h, win_ref[:, 0:RET_WIDTH])
    for c in range(n_blk):
        sl = slice(c * LANES, (c + 1) * LANES)
        qr_ref[:, sl] = rope_ret(zq[:, sl]).astype(BF16)
    zk = _dot(h, win_ref[:, RET_WIDTH:2 * RET_WIDTH])
    k_scale = RET_HEAD_DIM ** -0.5
    for c in range(n_blk):
        sl = slice(c * LANES, (c + 1) * LANES)
        kr_ref[:, sl] = (rope_ret(zk[:, sl]) * k_scale).astype(BF16)
    vr_ref[...] = _dot(h, win_ref[:, 2 * RET_WIDTH:3 * RET_WIDTH]).astype(BF16)
    zg = _dot(h, win_ref[:, 3 * RET_WIDTH:4 * RET_WIDTH])
    gr_ref[...] = jax.nn.silu(zg).astype(BF16)

    zc = _dot(h, win_ref[:, 4 * RET_WIDTH:IN_PAD])
    cq = _rms(zc[:, :MLA_Q_RANK], gqa_ref[...])
    ckv = _rms(zc[:, MLA_Q_RANK:MLA_Q_RANK + MLA_KV_RANK], gkva_ref[...])
    k_rot = rope_mla(zc[:, MLA_Q_RANK + MLA_KV_RANK:])

    k_up = _dot(ckv.astype(BF16), wuk_ref[...])
    for hh in range(MLA_HEADS):
        sl = slice(hh * LANES, (hh + 1) * LANES)
        km_ref[:, sl] = (k_up[:, sl] + k_rot).astype(BF16)

    vt_ref[...] = _dot(wuvt_ref[...], ckv.T.astype(BF16)).astype(BF16)

    q_t = _dot(wuqt_ref[...], cq.T.astype(BF16))
    cos_mt = cos_t[N_FREQ_RET:]
    sin_mt = sin_t[N_FREQ_RET:]
    q_scale = MLA_QK_DIM ** -0.5 * LOG2_E
    pad = jnp.zeros((LANES - r2, tm), F32)
    for hh in range(MLA_HEADS):
        b0 = hh * LANES
        x1 = q_t[b0 + r0:b0 + r1]
        x2 = q_t[b0 + r1:b0 + r2]
        blk = jnp.concatenate([q_t[b0:b0 + r0], x1 * cos_mt - x2 * sin_mt, x2 * cos_mt + x1 * sin_mt, pad],
                              axis=0)
        qt_ref[b0:b0 + LANES, :] = (blk * q_scale).astype(BF16)


def _proj(pos3, x2d, invf, g_pre, w_in_p, g_qa, w_uq_t, g_kva, w_uk_p, w_uv_t):
    t = x2d.shape[0]
    tm = TM_PROJ
    row = lambda i: (i, 0)
    col = lambda i: (0, i)
    row_widths = (RET_WIDTH, RET_WIDTH, RET_WIDTH, RET_WIDTH, MLA_PAD)
    return pl.pallas_call(
        _proj_kernel,
        grid=(t // tm,),
        in_specs=[
            pl.BlockSpec((1, 1, tm), lambda i: (i, 0, 0)),
            pl.BlockSpec((tm, D_MODEL), row),
            _const_spec(invf.shape),
            _const_spec(g_pre.shape),
            _const_spec(w_in_p.shape),
            _const_spec(g_qa.shape),
            _const_spec(w_uq_t.shape),
            _const_spec(g_kva.shape),
            _const_spec(w_uk_p.shape),
            _const_spec(w_uv_t.shape),
        ],
        out_specs=[pl.BlockSpec((tm, w), row) for w in row_widths] + [
            pl.BlockSpec((MLA_PAD, tm), col),
            pl.BlockSpec((MLA_WIDTH, tm), col),
        ],
        out_shape=[jax.ShapeDtypeStruct((t, w), BF16) for w in row_widths] + [
            jax.ShapeDtypeStruct((MLA_PAD, t), BF16),
            jax.ShapeDtypeStruct((MLA_WIDTH, t), BF16),
        ],
        compiler_params=pltpu.CompilerParams(
            dimension_semantics=("parallel",), vmem_limit_bytes=VMEM_LIMIT),
        name="proj",
    )(pos3, x2d, invf, g_pre, w_in_p, g_qa, w_uq_t, g_kva, w_uk_p, w_uv_t)


def _ret_kernel(lg_ref, q_ref, k_ref, v_ref, g_ref, gn_ref, o_ref):
    hp = pl.program_id(1)
    c = RET_CHUNK
    d = RET_HEAD_DIM
    lg0 = lg_ref[2 * hp]
    lg1 = lg_ref[2 * hp + 1]
    lane = lax.broadcasted_iota(jnp.int32, (1, LANES), 1)
    head0 = lane < d
    lgl = jnp.where(head0, lg0, lg1)
    ri = lax.broadcasted_iota(jnp.int32, (c, c), 0)
    ci = lax.broadcasted_iota(jnp.int32, (c, c), 1)
    rel = (ri - ci).astype(F32)
    causal = rel >= 0
    relp = jnp.maximum(rel, 0.0)
    intra0 = jnp.where(causal, jnp.exp(lg0 * relp), 0.0)
    intra1 = jnp.where(causal, jnp.exp(lg1 * relp), 0.0)
    rowf = lax.broadcasted_iota(jnp.int32, (c, 1), 0).astype(F32)
    inner = jnp.exp(lgl * (rowf + 1.0))
    tail = jnp.exp(lgl * (c - 1.0 - rowf))
    decay = jnp.exp(lgl * float(c))
    same_head = (ri < d) == (ci < d)
    gn = gn_ref[...]
    inv_d = 1.0 / d

    def step(i, state):
        sl = pl.ds(pl.multiple_of(i * c, c), c)
        q = q_ref[sl, :]
        k = k_ref[sl, :]
        v = v_ref[sl, :]
        cross = _dot(q, state.astype(BF16)) * inner
        zero = jnp.zeros_like(q)
        s0 = _dot_nt(jnp.where(head0, q, zero), k) * intra0
        s1 = _dot_nt(jnp.where(head0, zero, q), k) * intra1
        w0 = _dot(s0.astype(BF16), v)
        w1 = _dot(s1.astype(BF16), v)
        o = cross + jnp.where(head0, w0, w1)
        kt = (k.astype(F32) * tail).astype(BF16)
        upd = _dot_tn(kt, v)
        new_state = state * decay + jnp.where(same_head, upd, 0.0)

        mu0 = jnp.sum(jnp.where(head0, o, 0.0), axis=-1, keepdims=True)
        mu1 = jnp.sum(jnp.where(head0, 0.0, o), axis=-1, keepdims=True)
        dlt = o - jnp.where(head0, mu0, mu1) * inv_d
        sq = dlt * dlt
        v0 = jnp.sum(jnp.where(head0, sq, 0.0), axis=-1, keepdims=True)
        v1 = jnp.sum(jnp.where(head0, 0.0, sq), axis=-1, keepdims=True)
        var = jnp.where(head0, v0, v1) * inv_d
        on = dlt * lax.rsqrt(var + EPS) * gn
        o_ref[sl, :] = (g_ref[sl, :].astype(F32) * on).astype(BF16)
        return new_state

    n = q_ref.shape[0] // c
    lax.fori_loop(0, n, step, jnp.zeros((LANES, LANES), F32))


def _retention(log_gamma, q_r, k_r, v_r, g_r, g_gn, batch, seq):
    t = q_r.shape[0]
    blk = pl.BlockSpec((seq, LANES), lambda b, hp: (b, hp))
    return pl.pallas_call(
        _ret_kernel,
        grid=(batch, RET_WIDTH // LANES),
        in_specs=[
            pl.BlockSpec(memory_space=pltpu.SMEM),
            blk, blk, blk, blk,
            pl.BlockSpec((1, LANES), lambda b, hp: (0, hp)),
        ],
        out_specs=blk,
        out_shape=jax.ShapeDtypeStruct((t, RET_WIDTH), BF16),
        compiler_params=pltpu.CompilerParams(
            dimension_semantics=("parallel", "parallel"), vmem_limit_bytes=VMEM_LIMIT),
        name="retention",
    )(log_gamma, q_r, k_r, v_r, g_r, g_gn)


def _mla_kernel(qt_ref, k_ref, vt_ref, o_ref, m_ref, l_ref, acc_ref):
    tile = TQ_MLA
    n_tiles = k_ref.shape[0] // tile
    key = lax.broadcasted_iota(jnp.int32, (tile, tile), 0)
    qry = lax.broadcasted_iota(jnp.int32, (tile, tile), 1)
    diag_ok = key <= qry
    heads = (0, 1)

    for d in range(n_tiles):
        rows = slice(d * tile, (d + 1) * tile)
        done = slice(d * tile, (d + 1) * tile)
        live = slice(d * tile, None)
        for hh in heads:
            s = _dot(k_ref[rows, hh * LANES:(hh + 1) * LANES], qt_ref[hh * LANES:(hh + 1) * LANES, live])
            s_diag = jnp.where(diag_ok, s[:, :tile], -jnp.inf)
            s = s_diag if d == n_tiles - 1 else jnp.concatenate([s_diag, s[:, tile:]], axis=1)
            v_t = vt_ref[hh * MLA_V_DIM:(hh + 1) * MLA_V_DIM, rows]
            m_t = jnp.max(s, axis=0, keepdims=True)
            if d == 0:
                p = jnp.exp2(s - m_t)
                m_ref[hh] = m_t
                l_ref[hh] = jnp.sum(p, axis=0, keepdims=True)
                acc_ref[hh] = _dot(v_t, p.astype(BF16))
            else:
                m = m_ref[hh, :, live]
                m_new = jnp.maximum(m, m_t)
                alpha = jnp.exp2(m - m_new)
                p = jnp.exp2(s - m_new)
                m_ref[hh, :, live] = m_new
                l_ref[hh, :, live] = alpha * l_ref[hh, :, live] + jnp.sum(p, axis=0, keepdims=True)
                acc_ref[hh, :, live] = alpha * acc_ref[hh, :, live] + _dot(v_t, p.astype(BF16))
        o_t = jnp.concatenate([acc_ref[hh, :, done] / l_ref[hh, :, done] for hh in heads], axis=0)
        o_ref[done, :] = o_t.T.astype(BF16)


def _mla(q_t, k_m, v_t, batch, seq):
    t = k_m.shape[0]
    return pl.pallas_call(
        _mla_kernel,
        grid=(batch, MLA_WIDTH // LANES),
        in_specs=[
            pl.BlockSpec((2 * LANES, seq), lambda b, hp: (hp, b)),
            pl.BlockSpec((seq, 2 * LANES), lambda b, hp: (b, hp)),
            pl.BlockSpec((LANES, seq), lambda b, hp: (hp, b)),
        ],
        out_specs=pl.BlockSpec((seq, LANES), lambda b, hp: (b, hp)),
        out_shape=jax.ShapeDtypeStruct((t, MLA_WIDTH), BF16),
        scratch_shapes=[
            pltpu.VMEM((2, 1, seq), F32),
            pltpu.VMEM((2, 1, seq), F32),
            pltpu.VMEM((2, MLA_V_DIM, seq), F32),
        ],
        compiler_params=pltpu.CompilerParams(
            dimension_semantics=("parallel", "parallel"), vmem_limit_bytes=VMEM_LIMIT),
        name="mla_attention",
    )(q_t, k_m, v_t)


def _causal_conv(u, cw, halo):
    row = lax.broadcasted_iota(jnp.int32, halo.shape, 0)

    def shifted(n):
        r = pltpu.roll(u, n, 0)
        first = jnp.where(row < n, pltpu.roll(halo, n, 0), r[:SUBLANES])
        return jnp.concatenate([first, r[SUBLANES:]], axis=0)

    return cw[0:1] * shifted(2) + cw[1:2] * shifted(1) + cw[2:3] * u + cw[3:4]


def _post_kernel(x_ref, yr_ref, ym_ref, p_ref, wo_ref, gpm_ref, gpf_ref, wup_ref, cw_ref, wdn_ref,
                 gff_ref, wple_ref, wgate_ref, bgate_ref, gple_ref, out_ref,
                 halo_ref, x1_ref, h2_ref, f_ref):
    tm = x_ref.shape[0]

    @pl.when(pl.program_id(1) == 0)
    def _():
        halo_ref[...] = jnp.zeros_like(halo_ref)

    mix = _dot(yr_ref[...], wo_ref[0:RET_WIDTH, :]) + _dot(ym_ref[...], wo_ref[RET_WIDTH:, :])
    x1 = x_ref[...] + _rms(mix, gpm_ref[...])
    x1_ref[...] = x1
    h2_ref[...] = _rms(x1, gpf_ref[...]).astype(BF16)
    f_ref[...] = jnp.zeros_like(f_ref)

    def chunk(c, carry):
        hb = h2_ref[...]
        ug = _dot(hb, wup_ref[c])
        uu = _dot(hb, wup_ref[c + N_FF_CHUNKS])
        gate = _causal_conv(ug, cw_ref[c], halo_ref[c])
        up = _causal_conv(uu, cw_ref[c + N_FF_CHUNKS], halo_ref[c + N_FF_CHUNKS])
        halo_ref[c] = ug[tm - SUBLANES:, :]
        halo_ref[c + N_FF_CHUNKS] = uu[tm - SUBLANES:, :]
        act = (jax.nn.gelu(gate, approximate=True) * up).astype(BF16)
        f_ref[...] += _dot(act, wdn_ref[c])
        return carry

    lax.fori_loop(0, N_FF_CHUNKS, chunk, 0)

    x2 = x1_ref[...] + _rms(f_ref[...], gff_ref[...])
    gate = jax.nn.sigmoid(_dot(x2.astype(BF16), wgate_ref[...]) + bgate_ref[...])
    e = _dot(p_ref[...].astype(BF16), wple_ref[...]) * gate
    out_ref[...] = x2 + _rms(e, gple_ref[...])


def _post(x2d, y_ret, y_mla, p2d, w_o, g_pm, g_pf, w_up_c, cw_c, w_dn_c, g_ff, w_ple, w_gate, b_gate,
          g_ple, batch, seq):
    t = x2d.shape[0]
    tm = TM_POST
    ns = seq // tm
    row = lambda b, s: (b * ns + s, 0)
    consts = (w_o, g_pm, g_pf, w_up_c, cw_c, w_dn_c, g_ff, w_ple, w_gate, b_gate, g_ple)
    return pl.pallas_call(
        _post_kernel,
        grid=(batch, ns),
        in_specs=[
            pl.BlockSpec((tm, D_MODEL), row),
            pl.BlockSpec((tm, RET_WIDTH), row),
            pl.BlockSpec((tm, MLA_WIDTH), row),
            pl.BlockSpec((tm, PLE_DIM), row),
        ] + [_const_spec(a.shape) for a in consts],
        out_specs=pl.BlockSpec((tm, D_MODEL), row),
        out_shape=jax.ShapeDtypeStruct((t, D_MODEL), F32),
        scratch_shapes=[
            pltpu.VMEM((2 * N_FF_CHUNKS, SUBLANES, FF_CHUNK), F32),
            pltpu.VMEM((tm, D_MODEL), F32),
            pltpu.VMEM((tm, D_MODEL), BF16),
            pltpu.VMEM((tm, D_MODEL), F32),
        ],
        compiler_params=pltpu.CompilerParams(
            dimension_semantics=("arbitrary", "arbitrary"), vmem_limit_bytes=VMEM_LIMIT),
        name="post",
    )(x2d, y_ret, y_mla, p2d, *consts)


def _layer(x2d, p2d, pos3, batch, seq, w_in, w_uq, w_ukv, w_o, g_pre_mix, g_post_mix, g_q_a, g_kv_a,
           g_ret_gn, w_up, conv_w, conv_b, w_down, g_pre_ffn, g_post_ffn, w_ple, w_ple_gate,
           b_ple_gate, g_post_ple):
    row = lambda v: v.reshape(1, -1)

    kpe_blk = jnp.pad(w_in[:, IN_USED:], ((0, 0), (MLA_NOPE_DIM, LANES - MLA_QK_DIM)))
    w_in_p = jnp.concatenate([w_in[:, :IN_USED], kpe_blk], axis=1).astype(BF16)
    w_uq_t = jnp.pad(w_uq.reshape(MLA_Q_RANK, MLA_HEADS, MLA_QK_DIM),
                     ((0, 0), (0, 0), (0, LANES - MLA_QK_DIM))).reshape(MLA_Q_RANK, MLA_PAD).T.astype(BF16)
    w_ukv3 = w_ukv.reshape(MLA_KV_RANK, MLA_HEADS, MLA_NOPE_DIM + MLA_V_DIM)
    w_uk_p = jnp.pad(w_ukv3[..., :MLA_NOPE_DIM],
                     ((0, 0), (0, 0), (0, LANES - MLA_NOPE_DIM))).reshape(MLA_KV_RANK, MLA_PAD).astype(BF16)
    w_uv_t = w_ukv3[..., MLA_NOPE_DIM:].reshape(MLA_KV_RANK, MLA_WIDTH).T.astype(BF16)
    w_up_c = w_up.reshape(D_MODEL, 2 * N_FF_CHUNKS, FF_CHUNK).transpose(1, 0, 2).astype(BF16)
    w_dn_c = w_down.reshape(N_FF_CHUNKS, FF_CHUNK, D_MODEL).astype(BF16)
    cw4 = jnp.concatenate([conv_w, conv_b[None, :]], axis=0)
    cw_c = jnp.pad(cw4, ((0, SUBLANES - 4), (0, 0))).reshape(
        SUBLANES, 2 * N_FF_CHUNKS, FF_CHUNK).transpose(1, 0, 2)

    invf_r = ROPE_BASE ** (-jnp.arange(0, RET_HEAD_DIM, 2, dtype=F32) / RET_HEAD_DIM)
    invf_m = ROPE_BASE ** (-jnp.arange(0, MLA_ROPE_DIM, 2, dtype=F32) / MLA_ROPE_DIM)
    invf = jnp.concatenate([invf_r, invf_m]).reshape(-1, 1)
    log_gamma = jnp.log1p(-jnp.exp2(-5.0 - jnp.arange(RET_HEADS, dtype=F32)))

    q_r, k_r, v_r, g_r, k_m, q_t, v_t = _proj(
        pos3, x2d, invf, row(g_pre_mix), w_in_p, row(g_q_a), w_uq_t, row(g_kv_a), w_uk_p, w_uv_t)
    y_ret = _retention(log_gamma, q_r, k_r, v_r, g_r, row(g_ret_gn), batch, seq)
    y_mla = _mla(q_t, k_m, v_t, batch, seq)
    return _post(x2d, y_ret, y_mla, p2d, w_o.astype(BF16), row(g_post_mix), row(g_pre_ffn), w_up_c, cw_c,
                 w_dn_c, row(g_post_ffn), w_ple.astype(BF16), w_ple_gate.astype(BF16), row(b_ple_gate),
                 row(g_post_ple), batch, seq)


def kernel(x, p, positions, w_in, w_uq, w_ukv, w_o, g_pre_mix, g_post_mix, g_q_a, g_kv_a, g_ret_gn,
           w_up, conv_w, conv_b, w_down, g_pre_ffn, g_post_ffn, w_ple, w_ple_gate, b_ple_gate,
           g_post_ple):
    batch, seq, _ = x.shape
    depth = w_in.shape[0]
    assert seq % TM_PROJ == 0 and seq % TM_POST == 0 and seq % TQ_MLA == 0 and seq % RET_CHUNK == 0
    t = batch * seq
    x2d = x.reshape(t, D_MODEL)
    pos3 = positions.reshape(t // TM_PROJ, 1, TM_PROJ)
    for i in range(depth):
        x2d = _layer(x2d, p[i].reshape(t, PLE_DIM), pos3, batch, seq, w_in[i], w_uq[i], w_ukv[i], w_o[i],
                     g_pre_mix[i], g_post_mix[i], g_q_a[i], g_kv_a[i], g_ret_gn[i], w_up[i], conv_w[i],
                     conv_b[i], w_down[i], g_pre_ffn[i], g_post_ffn[i], w_ple[i], w_ple_gate[i],
                     b_ple_gate[i], g_post_ple[i])
    return x2d.reshape(batch, seq, D_MODEL)
```

```python
import functools

import jax
import jax.numpy as jnp
from jax import lax
from jax.experimental import pallas as pl
from jax.experimental.pallas import tpu as pltpu

F32 = jnp.float32
BF16 = jnp.bfloat16

D_MODEL = 1024
PLE_DIM = 256
RET_HEADS = 8
RET_HEAD_DIM = 64
RET_WIDTH = RET_HEADS * RET_HEAD_DIM
RET_CHUNK = 128
MLA_HEADS = 8
MLA_NOPE_DIM = 64
MLA_ROPE_DIM = 32
MLA_QK_DIM = MLA_NOPE_DIM + MLA_ROPE_DIM
MLA_V_DIM = 64
MLA_Q_RANK = 256
MLA_KV_RANK = 128
MLA_WIDTH = MLA_HEADS * MLA_V_DIM
D_FF = 2816
ROPE_BASE = 10000.0
EPS = 1e-6
LOG2_E = 1.4426950408889634

LANES = 128
SUBLANES = 8
IN_USED = 4 * RET_WIDTH + MLA_Q_RANK + MLA_KV_RANK
IN_PAD = IN_USED + LANES
MLA_PAD = MLA_HEADS * LANES
N_FREQ_RET = RET_HEAD_DIM // 2
N_FREQ_MLA = MLA_ROPE_DIM // 2
FF_CHUNK = 256
N_FF_CHUNKS = D_FF // FF_CHUNK

TM_PROJ = 512
TM_POST = 512
TQ_MLA = 256
VMEM_LIMIT = 56 * 1024 * 1024


def _dot(a, b):
    return jnp.dot(a, b, preferred_element_type=F32)


def _dot_nt(a, b):
    return lax.dot_general(a, b, (((1,), (1,)), ((), ())), preferred_element_type=F32)


def _dot_tn(a, b):
    return lax.dot_general(a, b, (((0,), (0,)), ((), ())), preferred_element_type=F32)


def _rms(x, g):
    return x * lax.rsqrt(jnp.mean(x * x, axis=-1, keepdims=True) + EPS) * g


def _const_spec(shape):
    zeros = (0,) * len(shape)
    return pl.BlockSpec(shape, lambda *_: zeros, pipeline_mode=pl.Buffered(1))


def _proj_kernel(pos_ref, x_ref, invf_ref, gpre_ref, win_ref, gqa_ref, wuqt_ref, gkva_ref, wuk_ref, wuvt_ref,
                 wqt_ref, wvt_ref, kr_ref, gr_ref, km_ref, qrt_ref, vrt_ref, qt_ref, vt_ref):
    tm = x_ref.shape[0]
    h_f32 = _rms(x_ref[...], gpre_ref[...])
    h = h_f32.astype(BF16)

    ang = invf_ref[...] * pos_ref[0].astype(F32)
    cos_t = jnp.cos(ang)
    sin_t = jnp.sin(ang)
    reps_r = LANES // N_FREQ_RET
    reps_m = LANES // N_FREQ_MLA
    cos_r = jnp.concatenate([cos_t[:N_FREQ_RET]] * reps_r, axis=0).T
    sin_r = jnp.concatenate([sin_t[:N_FREQ_RET]] * reps_r, axis=0).T
    cos_m = jnp.concatenate([cos_t[N_FREQ_RET:]] * reps_m, axis=0).T
    sin_m = jnp.concatenate([sin_t[N_FREQ_RET:]] * reps_m, axis=0).T

    lane = lax.broadcasted_iota(jnp.int32, (1, LANES), 1)
    first_half = (lane & (RET_HEAD_DIM - 1)) < N_FREQ_RET
    sin_r_lo = jnp.where(first_half, -sin_r, 0.0)
    sin_r_hi = jnp.where(first_half, 0.0, sin_r)

    def rope_ret(blk):
        return (blk * cos_r + pltpu.roll(blk, LANES - N_FREQ_RET, 1) * sin_r_lo
                + pltpu.roll(blk, N_FREQ_RET, 1) * sin_r_hi)

    r0 = MLA_NOPE_DIM
    r1 = MLA_NOPE_DIM + N_FREQ_MLA
    r2 = MLA_NOPE_DIM + MLA_ROPE_DIM
    cos_m_f = jnp.where(lane < r0, 1.0, jnp.where(lane < r2, cos_m, 0.0))
    sin_m_lo = jnp.where((lane >= r0) & (lane < r1), -sin_m, 0.0)
    sin_m_hi = jnp.where((lane >= r1) & (lane < r2), sin_m, 0.0)

    def rope_mla(blk):
        return (blk * cos_m_f + pltpu.roll(blk, LANES - N_FREQ_MLA, 1) * sin_m_lo
                + pltpu.roll(blk, N_FREQ_MLA, 1) * sin_m_hi)

    h_t = h_f32.T.astype(BF16)
    q_rt = _dot(wqt_ref[...], h_t)
    cos_rt = cos_t[:N_FREQ_RET]
    sin_rt = sin_t[:N_FREQ_RET]
    for hh in range(RET_HEADS):
        b0 = hh * RET_HEAD_DIM
        x1 = q_rt[b0:b0 + N_FREQ_RET]
        x2 = q_rt[b0 + N_FREQ_RET:b0 + RET_HEAD_DIM]
        qrt_ref[b0:b0 + RET_HEAD_DIM, :] = jnp.concatenate(
            [x1 * cos_rt - x2 * sin_rt, x2 * cos_rt + x1 * sin_rt], axis=0).astype(BF16)
    vrt_ref[...] = _dot(wvt_ref[...], h_t).astype(BF16)

    n_blk = RET_WIDTH // LANES
    zk = _dot(h, win_ref[:, 0:RET_WIDTH])
    k_scale = RET_HEAD_DIM ** -0.5
    for c in range(n_blk):
        sl = slice(c * LANES, (c + 1) * LANES)
        kr_ref[:, sl] = (rope_ret(zk[:, sl]) * k_scale).astype(BF16)
    zg = _dot(h, win_ref[:, RET_WIDTH:2 * RET_WIDTH])
    gr_ref[...] = jax.nn.silu(zg).astype(BF16)

    zc = _dot(h, win_ref[:, 2 * RET_WIDTH:])
    cq = _rms(zc[:, :MLA_Q_RANK], gqa_ref[...])
    ckv = _rms(zc[:, MLA_Q_RANK:MLA_Q_RANK + MLA_KV_RANK], gkva_ref[...])
    k_rot = rope_mla(zc[:, MLA_Q_RANK + MLA_KV_RANK:])

    k_up = _dot(ckv.astype(BF16), wuk_ref[...])
    for hh in range(MLA_HEADS):
        sl = slice(hh * LANES, (hh + 1) * LANES)
        km_ref[:, sl] = (k_up[:, sl] + k_rot).astype(BF16)

    vt_ref[...] = _dot(wuvt_ref[...], ckv.T.astype(BF16)).astype(BF16)

    q_t = _dot(wuqt_ref[...], cq.T.astype(BF16))
    cos_mt = cos_t[N_FREQ_RET:]
    sin_mt = sin_t[N_FREQ_RET:]
    q_scale = MLA_QK_DIM ** -0.5 * LOG2_E
    pad = jnp.zeros((LANES - r2, tm), F32)
    for hh in range(MLA_HEADS):
        b0 = hh * LANES
        x1 = q_t[b0 + r0:b0 + r1]
        x2 = q_t[b0 + r1:b0 + r2]
        blk = jnp.concatenate([q_t[b0:b0 + r0], x1 * cos_mt - x2 * sin_mt, x2 * cos_mt + x1 * sin_mt, pad],
                              axis=0)
        qt_ref[b0:b0 + LANES, :] = (blk * q_scale).astype(BF16)


def _proj(pos3, x2d, invf, g_pre, w_in_p, g_qa, w_uq_t, g_kva, w_uk_p, w_uv_t, w_q_t, w_v_t):
    t = x2d.shape[0]
    tm = TM_PROJ
    row = lambda i: (i, 0)
    col = lambda i: (0, i)
    row_widths = (RET_WIDTH, RET_WIDTH, MLA_PAD)
    col_heights = (RET_WIDTH, RET_WIDTH, MLA_PAD, MLA_WIDTH)
    return pl.pallas_call(
        _proj_kernel,
        grid=(t // tm,),
        in_specs=[
            pl.BlockSpec((1, 1, tm), lambda i: (i, 0, 0)),
            pl.BlockSpec((tm, D_MODEL), row),
            _const_spec(invf.shape),
            _const_spec(g_pre.shape),
            _const_spec(w_in_p.shape),
            _const_spec(g_qa.shape),
            _const_spec(w_uq_t.shape),
            _const_spec(g_kva.shape),
            _const_spec(w_uk_p.shape),
            _const_spec(w_uv_t.shape),
            _const_spec(w_q_t.shape),
            _const_spec(w_v_t.shape),
        ],
        out_specs=([pl.BlockSpec((tm, w), row) for w in row_widths]
                   + [pl.BlockSpec((hgt, tm), col) for hgt in col_heights]),
        out_shape=([jax.ShapeDtypeStruct((t, w), BF16) for w in row_widths]
                   + [jax.ShapeDtypeStruct((hgt, t), BF16) for hgt in col_heights]),
        compiler_params=pltpu.CompilerParams(
            dimension_semantics=("parallel",), vmem_limit_bytes=VMEM_LIMIT),
        name="proj",
    )(pos3, x2d, invf, g_pre, w_in_p, g_qa, w_uq_t, g_kva, w_uk_p, w_uv_t, w_q_t, w_v_t)


def _ret_kernel(lg_ref, qt_ref, k_ref, vt_ref, g_ref, gn_ref, o_ref):
    hp = pl.program_id(1)
    c = RET_CHUNK
    d = RET_HEAD_DIM
    n = k_ref.shape[0] // c
    lg0 = lg_ref[2 * hp]
    lg1 = lg_ref[2 * hp + 1]
    lane = lax.broadcasted_iota(jnp.int32, (1, LANES), 1)
    sub = lax.broadcasted_iota(jnp.int32, (LANES, 1), 0)
    lg_lane = jnp.where(lane < d, lg0, lg1)
    lg_sub = jnp.where(sub < d, lg0, lg1)
    key = lax.broadcasted_iota(jnp.int32, (c, c), 0)
    qry = lax.broadcasted_iota(jnp.int32, (c, c), 1)
    rel = (qry - key).astype(F32)
    causal = rel >= 0
    relp = jnp.maximum(rel, 0.0)
    intra = (jnp.where(causal, jnp.exp(lg0 * relp), 0.0), jnp.where(causal, jnp.exp(lg1 * relp), 0.0))
    tok_lane = lax.broadcasted_iota(jnp.int32, (1, c), 1).astype(F32)
    tok_sub = lax.broadcasted_iota(jnp.int32, (c, 1), 0).astype(F32)
    inner_t = jnp.exp(lg_sub * (tok_lane + 1.0))
    tail = jnp.exp(lg_lane * (c - 1.0 - tok_sub))
    decay = jnp.exp(lg_sub * float(c))
    same_head = (key < d) == (qry < d)
    zeros = jnp.zeros((d, c), BF16)
    heads = (0, 1)
    chunks = range(n)

    def tok(i):
        return slice(i * c, (i + 1) * c)

    q_t = [qt_ref[:, tok(i)] for i in chunks]
    q_h = [(jnp.concatenate([q_t[i][:d], zeros], axis=0), jnp.concatenate([zeros, q_t[i][d:]], axis=0))
           for i in chunks]
    s_t = [[(_dot(k_ref[tok(i), :], q_h[i][hh]) * intra[hh]).astype(BF16) for hh in heads] for i in chunks]
    w_t = [jnp.concatenate([_dot(vt_ref[hh * d:(hh + 1) * d, tok(i)], s_t[i][hh]) for hh in heads], axis=0)
           for i in chunks]
    upd = [jnp.where(same_head,
                     _dot(vt_ref[:, tok(i)], (k_ref[tok(i), :].astype(F32) * tail).astype(BF16)), 0.0)
           for i in chunks]
    states = [jnp.zeros((LANES, LANES), F32)]
    for i in range(n - 1):
        states.append(states[i] * decay + upd[i])
    gn = gn_ref[...]
    for i in chunks:
        o_t = w_t[i] if i == 0 else _dot(states[i].astype(BF16), q_t[i]) * inner_t + w_t[i]
        normed = []
        for hh in heads:
            o_h = o_t[hh * d:(hh + 1) * d]
            dlt = o_h - jnp.mean(o_h, axis=0, keepdims=True)
            var = jnp.mean(dlt * dlt, axis=0, keepdims=True)
            normed.append(dlt * lax.rsqrt(var + EPS))
        on = jnp.concatenate(normed, axis=0).T
        o_ref[tok(i), :] = (g_ref[tok(i), :].astype(F32) * (on * gn)).astype(BF16)


def _retention(log_gamma, q_rt, k_r, v_rt, g_r, g_gn, batch, seq):
    t = k_r.shape[0]
    row_blk = pl.BlockSpec((seq, LANES), lambda b, hp: (b, hp))
    col_blk = pl.BlockSpec((LANES, seq), lambda b, hp: (hp, b))
    return pl.pallas_call(
        _ret_kernel,
        grid=(batch, RET_WIDTH // LANES),
        in_specs=[
            pl.BlockSpec(memory_space=pltpu.SMEM),
            col_blk, row_blk, col_blk, row_blk,
            pl.BlockSpec((1, LANES), lambda b, hp: (0, hp)),
        ],
        out_specs=row_blk,
        out_shape=jax.ShapeDtypeStruct((t, RET_WIDTH), BF16),
        compiler_params=pltpu.CompilerParams(
            dimension_semantics=("parallel", "parallel"), vmem_limit_bytes=VMEM_LIMIT),
        name="retention",
    )(log_gamma, q_rt, k_r, v_rt, g_r, g_gn)


def _mla_kernel(qt_ref, k_ref, vt_ref, o_ref, m_ref, l_ref, acc_ref):
    tile = TQ_MLA
    n_tiles = k_ref.shape[0] // tile
    key = lax.broadcasted_iota(jnp.int32, (tile, tile), 0)
    qry = lax.broadcasted_iota(jnp.int32, (tile, tile), 1)
    diag_ok = key <= qry
    heads = (0, 1)

    for d in range(n_tiles):
        rows = slice(d * tile, (d + 1) * tile)
        done = slice(d * tile, (d + 1) * tile)
        live = slice(d * tile, None)
        for hh in heads:
            s = _dot(k_ref[rows, hh * LANES:(hh + 1) * LANES], qt_ref[hh * LANES:(hh + 1) * LANES, live])
            s_diag = jnp.where(diag_ok, s[:, :tile], -jnp.inf)
            s = s_diag if d == n_tiles - 1 else jnp.concatenate([s_diag, s[:, tile:]], axis=1)
            v_t = vt_ref[hh * MLA_V_DIM:(hh + 1) * MLA_V_DIM, rows]
            m_t = jnp.max(s, axis=0, keepdims=True)
            if d == 0:
                p = jnp.exp2(s - m_t)
                m_ref[hh] = m_t
                l_ref[hh] = jnp.sum(p, axis=0, keepdims=True)
                acc_ref[hh] = _dot(v_t, p.astype(BF16))
            else:
                m = m_ref[hh, :, live]
                m_new = jnp.maximum(m, m_t)
                alpha = jnp.exp2(m - m_new)
                p = jnp.exp2(s - m_new)
                m_ref[hh, :, live] = m_new
                l_ref[hh, :, live] = alpha * l_ref[hh, :, live] + jnp.sum(p, axis=0, keepdims=True)
                acc_ref[hh, :, live] = alpha * acc_ref[hh, :, live] + _dot(v_t, p.astype(BF16))
        o_t = jnp.concatenate([acc_ref[hh, :, done] / l_ref[hh, :, done] for hh in heads], axis=0)
        o_ref[done, :] = o_t.T.astype(BF16)


def _mla(q_t, k_m, v_t, batch, seq):
    t = k_m.shape[0]
    return pl.pallas_call(
        _mla_kernel,
        grid=(batch, MLA_WIDTH // LANES),
        in_specs=[
            pl.BlockSpec((2 * LANES, seq), lambda b, hp: (hp, b)),
            pl.BlockSpec((seq, 2 * LANES), lambda b, hp: (b, hp)),
            pl.BlockSpec((LANES, seq), lambda b, hp: (hp, b)),
        ],
        out_specs=pl.BlockSpec((seq, LANES), lambda b, hp: (b, hp)),
        out_shape=jax.ShapeDtypeStruct((t, MLA_WIDTH), BF16),
        scratch_shapes=[
            pltpu.VMEM((2, 1, seq), F32),
            pltpu.VMEM((2, 1, seq), F32),
            pltpu.VMEM((2, MLA_V_DIM, seq), F32),
        ],
        compiler_params=pltpu.CompilerParams(
            dimension_semantics=("parallel", "parallel"), vmem_limit_bytes=VMEM_LIMIT),
        name="mla_attention",
    )(q_t, k_m, v_t)


def _causal_conv(u, cw, halo):
    row = lax.broadcasted_iota(jnp.int32, halo.shape, 0)

    def shifted(n):
        r = pltpu.roll(u, n, 0)
        first = jnp.where(row < n, pltpu.roll(halo, n, 0), r[:SUBLANES])
        return jnp.concatenate([first, r[SUBLANES:]], axis=0)

    return cw[0:1] * shifted(2) + cw[1:2] * shifted(1) + cw[2:3] * u + cw[3:4]


def _post_kernel(x_ref, yr_ref, ym_ref, p_ref, wo_ref, gpm_ref, gpf_ref, wup_ref, cw_ref, wdn_ref,
                 gff_ref, wple_ref, wgate_ref, bgate_ref, gple_ref, out_ref,
                 halo_ref, x1_ref, h2_ref, f_ref):
    tm = x_ref.shape[0]

    @pl.when(pl.program_id(1) == 0)
    def _():
        halo_ref[...] = jnp.zeros_like(halo_ref)

    mix = _dot(yr_ref[...], wo_ref[0:RET_WIDTH, :]) + _dot(ym_ref[...], wo_ref[RET_WIDTH:, :])
    x1 = x_ref[...] + _rms(mix, gpm_ref[...])
    x1_ref[...] = x1
    h2_ref[...] = _rms(x1, gpf_ref[...]).astype(BF16)
    e_in = _dot(p_ref[...].astype(BF16), wple_ref[...])

    def up_proj(c):
        hb = h2_ref[...]
        return _dot(hb, wup_ref[c]), _dot(hb, wup_ref[c + N_FF_CHUNKS])

    nxt = up_proj(0)
    for c in range(N_FF_CHUNKS):
        ug, uu = nxt
        if c + 1 < N_FF_CHUNKS:
            nxt = up_proj(c + 1)
        gate = _causal_conv(ug, cw_ref[c], halo_ref[c])
        up = _causal_conv(uu, cw_ref[c + N_FF_CHUNKS], halo_ref[c + N_FF_CHUNKS])
        halo_ref[c] = ug[tm - SUBLANES:, :]
        halo_ref[c + N_FF_CHUNKS] = uu[tm - SUBLANES:, :]
        act = (jax.nn.gelu(gate, approximate=True) * up).astype(BF16)
        dn = _dot(act, wdn_ref[c])
        if c == 0:
            f_ref[...] = dn
        else:
            f_ref[...] += dn

    x2 = x1_ref[...] + _rms(f_ref[...], gff_ref[...])
    gate = jax.nn.sigmoid(_dot(x2.astype(BF16), wgate_ref[...]) + bgate_ref[...])
    out_ref[...] = x2 + _rms(e_in * gate, gple_ref[...])


def _post(x2d, y_ret, y_mla, p2d, w_o, g_pm, g_pf, w_up_c, cw_c, w_dn_c, g_ff, w_ple, w_gate, b_gate,
          g_ple, batch, seq):
    t = x2d.shape[0]
    tm = TM_POST
    ns = seq // tm
    row = lambda b, s: (b * ns + s, 0)
    consts = (w_o, g_pm, g_pf, w_up_c, cw_c, w_dn_c, g_ff, w_ple, w_gate, b_gate, g_ple)
    return pl.pallas_call(
        _post_kernel,
        grid=(batch, ns),
        in_specs=[
            pl.BlockSpec((tm, D_MODEL), row),
            pl.BlockSpec((tm, RET_WIDTH), row),
            pl.BlockSpec((tm, MLA_WIDTH), row),
            pl.BlockSpec((tm, PLE_DIM), row),
        ] + [_const_spec(a.shape) for a in consts],
        out_specs=pl.BlockSpec((tm, D_MODEL), row),
        out_shape=jax.ShapeDtypeStruct((t, D_MODEL), F32),
        scratch_shapes=[
            pltpu.VMEM((2 * N_FF_CHUNKS, SUBLANES, FF_CHUNK), F32),
            pltpu.VMEM((tm, D_MODEL), F32),
            pltpu.VMEM((tm, D_MODEL), BF16),
            pltpu.VMEM((tm, D_MODEL), F32),
        ],
        compiler_params=pltpu.CompilerParams(
            dimension_semantics=("arbitrary", "arbitrary"), vmem_limit_bytes=VMEM_LIMIT),
        name="post",
    )(x2d, y_ret, y_mla, p2d, *consts)


def _layer(x2d, p2d, pos3, batch, seq, w_in, w_uq, w_ukv, w_o, g_pre_mix, g_post_mix, g_q_a, g_kv_a,
           g_ret_gn, w_up, conv_w, conv_b, w_down, g_pre_ffn, g_post_ffn, w_ple, w_ple_gate,
           b_ple_gate, g_post_ple):
    row = lambda v: v.reshape(1, -1)

    kpe_blk = jnp.pad(w_in[:, IN_USED:], ((0, 0), (MLA_NOPE_DIM, LANES - MLA_QK_DIM)))
    w_in_p = jnp.concatenate([w_in[:, RET_WIDTH:2 * RET_WIDTH], w_in[:, 3 * RET_WIDTH:IN_USED], kpe_blk],
                             axis=1).astype(BF16)
    w_q_t = w_in[:, :RET_WIDTH].T.astype(BF16)
    w_v_t = w_in[:, 2 * RET_WIDTH:3 * RET_WIDTH].T.astype(BF16)
    w_uq_t = jnp.pad(w_uq.reshape(MLA_Q_RANK, MLA_HEADS, MLA_QK_DIM),
                     ((0, 0), (0, 0), (0, LANES - MLA_QK_DIM))).reshape(MLA_Q_RANK, MLA_PAD).T.astype(BF16)
    w_ukv3 = w_ukv.reshape(MLA_KV_RANK, MLA_HEADS, MLA_NOPE_DIM + MLA_V_DIM)
    w_uk_p = jnp.pad(w_ukv3[..., :MLA_NOPE_DIM],
                     ((0, 0), (0, 0), (0, LANES - MLA_NOPE_DIM))).reshape(MLA_KV_RANK, MLA_PAD).astype(BF16)
    w_uv_t = w_ukv3[..., MLA_NOPE_DIM:].reshape(MLA_KV_RANK, MLA_WIDTH).T.astype(BF16)
    w_up_c = w_up.reshape(D_MODEL, 2 * N_FF_CHUNKS, FF_CHUNK).transpose(1, 0, 2).astype(BF16)
    w_dn_c = w_down.reshape(N_FF_CHUNKS, FF_CHUNK, D_MODEL).astype(BF16)
    cw4 = jnp.concatenate([conv_w, conv_b[None, :]], axis=0)
    cw_c = jnp.pad(cw4, ((0, SUBLANES - 4), (0, 0))).reshape(
        SUBLANES, 2 * N_FF_CHUNKS, FF_CHUNK).transpose(1, 0, 2)

    invf_r = ROPE_BASE ** (-jnp.arange(0, RET_HEAD_DIM, 2, dtype=F32) / RET_HEAD_DIM)
    invf_m = ROPE_BASE ** (-jnp.arange(0, MLA_ROPE_DIM, 2, dtype=F32) / MLA_ROPE_DIM)
    invf = jnp.concatenate([invf_r, invf_m]).reshape(-1, 1)
    log_gamma = jnp.log1p(-jnp.exp2(-5.0 - jnp.arange(RET_HEADS, dtype=F32)))

    k_r, g_r, k_m, q_rt, v_rt, q_t, v_t = _proj(
        pos3, x2d, invf, row(g_pre_mix), w_in_p, row(g_q_a), w_uq_t, row(g_kv_a), w_uk_p, w_uv_t, w_q_t, w_v_t)
    y_ret = _retention(log_gamma, q_rt, k_r, v_rt, g_r, row(g_ret_gn), batch, seq)
    y_mla = _mla(q_t, k_m, v_t, batch, seq)
    return _post(x2d, y_ret, y_mla, p2d, w_o.astype(BF16), row(g_post_mix), row(g_pre_ffn), w_up_c, cw_c,
                 w_dn_c, row(g_post_ffn), w_ple.astype(BF16), w_ple_gate.astype(BF16), row(b_ple_gate),
                 row(g_post_ple), batch, seq)


def kernel(x, p, positions, w_in, w_uq, w_ukv, w_o, g_pre_mix, g_post_mix, g_q_a, g_kv_a, g_ret_gn,
           w_up, conv_w, conv_b, w_down, g_pre_ffn, g_post_ffn, w_ple, w_ple_gate, b_ple_gate,
           g_post_ple):
    batch, seq, _ = x.shape
    depth = w_in.shape[0]
    assert seq % TM_PROJ == 0 and seq % TM_POST == 0 and seq % TQ_MLA == 0 and seq % RET_CHUNK == 0
    t = batch * seq
    x2d = x.reshape(t, D_MODEL)
    pos3 = positions.reshape(t // TM_PROJ, 1, TM_PROJ)
    for i in range(depth):
        x2d = _layer(x2d, p[i].reshape(t, PLE_DIM), pos3, batch, seq, w_in[i], w_uq[i], w_ukv[i], w_o[i],
                     g_pre_mix[i], g_post_mix[i], g_q_a[i], g_kv_a[i], g_ret_gn[i], w_up[i], conv_w[i],
                     conv_b[i], w_down[i], g_pre_ffn[i], g_post_ffn[i], w_ple[i], w_ple_gate[i],
                     b_ple_gate[i], g_post_ple[i])
    return x2d.reshape(batch, seq, D_MODEL)
```

```python
import functools

import jax
import jax.numpy as jnp
from jax import lax
from jax.experimental import pallas as pl
from jax.experimental.pallas import tpu as pltpu

F32 = jnp.float32
BF16 = jnp.bfloat16

D_MODEL = 1024
PLE_DIM = 256
RET_HEADS = 8
RET_HEAD_DIM = 64
RET_WIDTH = RET_HEADS * RET_HEAD_DIM
RET_CHUNK = 128
MLA_HEADS = 8
MLA_NOPE_DIM = 64
MLA_ROPE_DIM = 32
MLA_QK_DIM = MLA_NOPE_DIM + MLA_ROPE_DIM
MLA_V_DIM = 64
MLA_Q_RANK = 256
MLA_KV_RANK = 128
MLA_WIDTH = MLA_HEADS * MLA_V_DIM
D_FF = 2816
ROPE_BASE = 10000.0
EPS = 1e-6
LOG2_E = 1.4426950408889634

LANES = 128
SUBLANES = 8
IN_USED = 4 * RET_WIDTH + MLA_Q_RANK + MLA_KV_RANK
IN_PAD = IN_USED + LANES
MLA_PAD = MLA_HEADS * LANES
N_FREQ_RET = RET_HEAD_DIM // 2
N_FREQ_MLA = MLA_ROPE_DIM // 2
FF_CHUNK = 256
N_FF_CHUNKS = D_FF // FF_CHUNK

TM_PROJ = 512
TM_POST = 512
ROWS_POST = 128
TQ_MLA = 256
ONES_ROWS = 16
VMEM_LIMIT = 56 * 1024 * 1024


def _dot(a, b):
    return jnp.dot(a, b, preferred_element_type=F32)


def _dot_nt(a, b):
    return lax.dot_general(a, b, (((1,), (1,)), ((), ())), preferred_element_type=F32)


def _dot_tn(a, b):
    return lax.dot_general(a, b, (((0,), (0,)), ((), ())), preferred_element_type=F32)


def _rms(x, g):
    return x * lax.rsqrt(jnp.mean(x * x, axis=-1, keepdims=True) + EPS) * g


def _const_spec(shape):
    zeros = (0,) * len(shape)
    return pl.BlockSpec(shape, lambda *_: zeros, pipeline_mode=pl.Buffered(1))


def _proj_kernel(pos_ref, x_ref, invf_ref, gpre_ref, win_ref, gqa_ref, wuqt_ref, gkva_ref, wuk_ref, wuvt_ref,
                 wqt_ref, wvt_ref, kr_ref, gr_ref, km_ref, qrt_ref, vrt_ref, qt_ref, vt_ref):
    tm = x_ref.shape[0]
    h_f32 = _rms(x_ref[...], gpre_ref[...])
    h = h_f32.astype(BF16)

    ang = invf_ref[...] * pos_ref[0].astype(F32)
    cos_t = jnp.cos(ang)
    sin_t = jnp.sin(ang)
    reps_r = LANES // N_FREQ_RET
    reps_m = LANES // N_FREQ_MLA
    cos_r = jnp.concatenate([cos_t[:N_FREQ_RET]] * reps_r, axis=0).T
    sin_r = jnp.concatenate([sin_t[:N_FREQ_RET]] * reps_r, axis=0).T
    cos_m = jnp.concatenate([cos_t[N_FREQ_RET:]] * reps_m, axis=0).T
    sin_m = jnp.concatenate([sin_t[N_FREQ_RET:]] * reps_m, axis=0).T

    lane = lax.broadcasted_iota(jnp.int32, (1, LANES), 1)
    first_half = (lane & (RET_HEAD_DIM - 1)) < N_FREQ_RET
    sin_r_lo = jnp.where(first_half, -sin_r, 0.0)
    sin_r_hi = jnp.where(first_half, 0.0, sin_r)

    def rope_ret(blk):
        return (blk * cos_r + pltpu.roll(blk, LANES - N_FREQ_RET, 1) * sin_r_lo
                + pltpu.roll(blk, N_FREQ_RET, 1) * sin_r_hi)

    r0 = MLA_NOPE_DIM
    r1 = MLA_NOPE_DIM + N_FREQ_MLA
    r2 = MLA_NOPE_DIM + MLA_ROPE_DIM
    cos_m_f = jnp.where(lane < r0, 1.0, jnp.where(lane < r2, cos_m, 0.0))
    sin_m_lo = jnp.where((lane >= r0) & (lane < r1), -sin_m, 0.0)
    sin_m_hi = jnp.where((lane >= r1) & (lane < r2), sin_m, 0.0)

    def rope_mla(blk):
        return (blk * cos_m_f + pltpu.roll(blk, LANES - N_FREQ_MLA, 1) * sin_m_lo
                + pltpu.roll(blk, N_FREQ_MLA, 1) * sin_m_hi)

    h_t = h_f32.T.astype(BF16)
    q_rt = _dot(wqt_ref[...], h_t)
    cos_rt = cos_t[:N_FREQ_RET]
    sin_rt = sin_t[:N_FREQ_RET]
    for hh in range(RET_HEADS):
        b0 = hh * RET_HEAD_DIM
        x1 = q_rt[b0:b0 + N_FREQ_RET]
        x2 = q_rt[b0 + N_FREQ_RET:b0 + RET_HEAD_DIM]
        qrt_ref[b0:b0 + RET_HEAD_DIM, :] = jnp.concatenate(
            [x1 * cos_rt - x2 * sin_rt, x2 * cos_rt + x1 * sin_rt], axis=0).astype(BF16)
    vrt_ref[...] = _dot(wvt_ref[...], h_t).astype(BF16)

    n_blk = RET_WIDTH // LANES
    zk = _dot(h, win_ref[:, 0:RET_WIDTH])
    k_scale = RET_HEAD_DIM ** -0.5
    for c in range(n_blk):
        sl = slice(c * LANES, (c + 1) * LANES)
        kr_ref[:, sl] = (rope_ret(zk[:, sl]) * k_scale).astype(BF16)
    zg = _dot(h, win_ref[:, RET_WIDTH:2 * RET_WIDTH])
    gr_ref[...] = jax.nn.silu(zg).astype(BF16)

    zc = _dot(h, win_ref[:, 2 * RET_WIDTH:])
    cq = _rms(zc[:, :MLA_Q_RANK], gqa_ref[...])
    ckv = _rms(zc[:, MLA_Q_RANK:MLA_Q_RANK + MLA_KV_RANK], gkva_ref[...])
    k_rot = rope_mla(zc[:, MLA_Q_RANK + MLA_KV_RANK:])

    k_up = _dot(ckv.astype(BF16), wuk_ref[...])
    for hh in range(MLA_HEADS):
        sl = slice(hh * LANES, (hh + 1) * LANES)
        km_ref[:, sl] = (k_up[:, sl] + k_rot).astype(BF16)

    vt_ref[...] = _dot(wuvt_ref[...], ckv.T.astype(BF16)).astype(BF16)

    q_t = _dot(wuqt_ref[...], cq.T.astype(BF16))
    cos_mt = cos_t[N_FREQ_RET:]
    sin_mt = sin_t[N_FREQ_RET:]
    q_scale = MLA_QK_DIM ** -0.5 * LOG2_E
    pad = jnp.zeros((LANES - r2, tm), F32)
    for hh in range(MLA_HEADS):
        b0 = hh * LANES
        x1 = q_t[b0 + r0:b0 + r1]
        x2 = q_t[b0 + r1:b0 + r2]
        blk = jnp.concatenate([q_t[b0:b0 + r0], x1 * cos_mt - x2 * sin_mt, x2 * cos_mt + x1 * sin_mt, pad],
                              axis=0)
        qt_ref[b0:b0 + LANES, :] = (blk * q_scale).astype(BF16)


def _proj(pos3, x2d, invf, g_pre, w_in_p, g_qa, w_uq_t, g_kva, w_uk_p, w_uv_t, w_q_t, w_v_t):
    t = x2d.shape[0]
    tm = TM_PROJ
    row = lambda i: (i, 0)
    col = lambda i: (0, i)
    row_widths = (RET_WIDTH, RET_WIDTH, MLA_PAD)
    col_heights = (RET_WIDTH, RET_WIDTH, MLA_PAD, MLA_WIDTH)
    return pl.pallas_call(
        _proj_kernel,
        grid=(t // tm,),
        in_specs=[
            pl.BlockSpec((1, 1, tm), lambda i: (i, 0, 0)),
            pl.BlockSpec((tm, D_MODEL), row),
            _const_spec(invf.shape),
            _const_spec(g_pre.shape),
            _const_spec(w_in_p.shape),
            _const_spec(g_qa.shape),
            _const_spec(w_uq_t.shape),
            _const_spec(g_kva.shape),
            _const_spec(w_uk_p.shape),
            _const_spec(w_uv_t.shape),
            _const_spec(w_q_t.shape),
            _const_spec(w_v_t.shape),
        ],
        out_specs=([pl.BlockSpec((tm, w), row) for w in row_widths]
                   + [pl.BlockSpec((hgt, tm), col) for hgt in col_heights]),
        out_shape=([jax.ShapeDtypeStruct((t, w), BF16) for w in row_widths]
                   + [jax.ShapeDtypeStruct((hgt, t), BF16) for hgt in col_heights]),
        compiler_params=pltpu.CompilerParams(
            dimension_semantics=("parallel",), vmem_limit_bytes=VMEM_LIMIT),
        name="proj",
    )(pos3, x2d, invf, g_pre, w_in_p, g_qa, w_uq_t, g_kva, w_uk_p, w_uv_t, w_q_t, w_v_t)


def _ret_kernel(lg_ref, qt_ref, k_ref, vt_ref, g_ref, gn_ref, o_ref):
    hp = pl.program_id(1)
    c = RET_CHUNK
    d = RET_HEAD_DIM
    n = k_ref.shape[0] // c
    lg0 = lg_ref[2 * hp]
    lg1 = lg_ref[2 * hp + 1]
    lane = lax.broadcasted_iota(jnp.int32, (1, LANES), 1)
    sub = lax.broadcasted_iota(jnp.int32, (LANES, 1), 0)
    lg_lane = jnp.where(lane < d, lg0, lg1)
    lg_sub = jnp.where(sub < d, lg0, lg1)
    key = lax.broadcasted_iota(jnp.int32, (c, c), 0)
    qry = lax.broadcasted_iota(jnp.int32, (c, c), 1)
    rel = (qry - key).astype(F32)
    causal = rel >= 0
    relp = jnp.maximum(rel, 0.0)
    intra = (jnp.where(causal, jnp.exp(lg0 * relp), 0.0), jnp.where(causal, jnp.exp(lg1 * relp), 0.0))
    tok_lane = lax.broadcasted_iota(jnp.int32, (1, c), 1).astype(F32)
    tok_sub = lax.broadcasted_iota(jnp.int32, (c, 1), 0).astype(F32)
    inner_t = jnp.exp(lg_sub * (tok_lane + 1.0))
    tail = jnp.exp(lg_lane * (c - 1.0 - tok_sub))
    decay = jnp.exp(lg_sub * float(c))
    same_head = (key < d) == (qry < d)
    zeros = jnp.zeros((d, c), BF16)
    heads = (0, 1)
    chunks = range(n)

    def tok(i):
        return slice(i * c, (i + 1) * c)

    q_t = [qt_ref[:, tok(i)] for i in chunks]
    q_h = [(jnp.concatenate([q_t[i][:d], zeros], axis=0), jnp.concatenate([zeros, q_t[i][d:]], axis=0))
           for i in chunks]
    s_t = [[(_dot(k_ref[tok(i), :], q_h[i][hh]) * intra[hh]).astype(BF16) for hh in heads] for i in chunks]
    w_t = [jnp.concatenate([_dot(vt_ref[hh * d:(hh + 1) * d, tok(i)], s_t[i][hh]) for hh in heads], axis=0)
           for i in chunks]
    upd = [jnp.where(same_head,
                     _dot(vt_ref[:, tok(i)], (k_ref[tok(i), :].astype(F32) * tail).astype(BF16)), 0.0)
           for i in chunks]
    states = [jnp.zeros((LANES, LANES), F32)]
    for i in range(n - 1):
        states.append(states[i] * decay + upd[i])
    gn = gn_ref[...]
    for i in chunks:
        o_t = w_t[i] if i == 0 else _dot(states[i].astype(BF16), q_t[i]) * inner_t + w_t[i]
        normed = []
        for hh in heads:
            o_h = o_t[hh * d:(hh + 1) * d]
            dlt = o_h - jnp.mean(o_h, axis=0, keepdims=True)
            var = jnp.mean(dlt * dlt, axis=0, keepdims=True)
            normed.append(dlt * lax.rsqrt(var + EPS))
        on = jnp.concatenate(normed, axis=0).T
        o_ref[tok(i), :] = (g_ref[tok(i), :].astype(F32) * (on * gn)).astype(BF16)


def _retention(log_gamma, q_rt, k_r, v_rt, g_r, g_gn, batch, seq):
    t = k_r.shape[0]
    row_blk = pl.BlockSpec((seq, LANES), lambda b, hp: (b, hp))
    col_blk = pl.BlockSpec((LANES, seq), lambda b, hp: (hp, b))
    return pl.pallas_call(
        _ret_kernel,
        grid=(batch, RET_WIDTH // LANES),
        in_specs=[
            pl.BlockSpec(memory_space=pltpu.SMEM),
            col_blk, row_blk, col_blk, row_blk,
            pl.BlockSpec((1, LANES), lambda b, hp: (0, hp)),
        ],
        out_specs=row_blk,
        out_shape=jax.ShapeDtypeStruct((t, RET_WIDTH), BF16),
        compiler_params=pltpu.CompilerParams(
            dimension_semantics=("parallel", "parallel"), vmem_limit_bytes=VMEM_LIMIT),
        name="retention",
    )(log_gamma, q_rt, k_r, v_rt, g_r, g_gn)


def _mla_kernel(qt_ref, k_ref, vt_ref, o_ref, m_ref, acc_ref):
    tile = TQ_MLA
    n_tiles = k_ref.shape[0] // tile
    key = lax.broadcasted_iota(jnp.int32, (tile, tile), 0)
    qry = lax.broadcasted_iota(jnp.int32, (tile, tile), 1)
    diag_ok = key <= qry
    heads = (0, 1)
    dv = MLA_V_DIM
    ones = jnp.ones((ONES_ROWS, tile), BF16)

    def scores(d, hh):
        return _dot(k_ref[d * tile:(d + 1) * tile, hh * LANES:(hh + 1) * LANES],
                    qt_ref[hh * LANES:(hh + 1) * LANES, d * tile:])

    def update(d, hh, s, cols):
        v_ext = jnp.concatenate([vt_ref[hh * dv:(hh + 1) * dv, d * tile:(d + 1) * tile], ones], axis=0)
        m_t = jnp.max(s, axis=0, keepdims=True)
        if d == 0:
            m_ref[hh, :, cols] = m_t
            acc_ref[hh, :, cols] = _dot(v_ext, jnp.exp2(s - m_t).astype(BF16))
        else:
            m = m_ref[hh, :, cols]
            m_new = jnp.maximum(m, m_t)
            m_ref[hh, :, cols] = m_new
            pv = _dot(v_ext, jnp.exp2(s - m_new).astype(BF16))
            acc_ref[hh, :, cols] = jnp.exp2(m - m_new) * acc_ref[hh, :, cols] + pv

    s_next = {hh: scores(0, hh) for hh in heads}
    for d in range(n_tiles):
        done = slice(d * tile, (d + 1) * tile)
        for hh in heads:
            s = s_next[hh]
            if d + 1 < n_tiles:
                s_next[hh] = scores(d + 1, hh)
            update(d, hh, jnp.where(diag_ok, s[:, :tile], -jnp.inf), done)
            if d + 1 < n_tiles:
                update(d, hh, s[:, tile:], slice((d + 1) * tile, None))
        o_t = jnp.concatenate([acc_ref[hh, :dv, done] / acc_ref[hh, dv:dv + 1, done] for hh in heads], axis=0)
        o_ref[done, :] = o_t.T.astype(BF16)


def _mla(q_t, k_m, v_t, batch, seq):
    t = k_m.shape[0]
    return pl.pallas_call(
        _mla_kernel,
        grid=(batch, MLA_WIDTH // LANES),
        in_specs=[
            pl.BlockSpec((2 * LANES, seq), lambda b, hp: (hp, b)),
            pl.BlockSpec((seq, 2 * LANES), lambda b, hp: (b, hp)),
            pl.BlockSpec((LANES, seq), lambda b, hp: (hp, b)),
        ],
        out_specs=pl.BlockSpec((seq, LANES), lambda b, hp: (b, hp)),
        out_shape=jax.ShapeDtypeStruct((t, MLA_WIDTH), BF16),
        scratch_shapes=[
            pltpu.VMEM((2, 1, seq), F32),
            pltpu.VMEM((2, MLA_V_DIM + ONES_ROWS, seq), F32),
        ],
        compiler_params=pltpu.CompilerParams(
            dimension_semantics=("parallel", "parallel"), vmem_limit_bytes=VMEM_LIMIT),
        name="mla_attention",
    )(q_t, k_m, v_t)


def _causal_conv(u, cw, halo):
    row = lax.broadcasted_iota(jnp.int32, halo.shape, 0)

    def shifted(n):
        r = pltpu.roll(u, n, 0)
        first = jnp.where(row < n, pltpu.roll(halo, n, 0), r[:SUBLANES])
        return jnp.concatenate([first, r[SUBLANES:]], axis=0)

    return cw[0:1] * shifted(2) + cw[1:2] * shifted(1) + cw[2:3] * u + cw[3:4]


def _post_kernel(x_ref, yr_ref, ym_ref, p_ref, wo_ref, gpm_ref, gpf_ref, wup_ref, cw_ref, wdn_ref,
                 gff_ref, wple_ref, wgate_ref, bgate_ref, gple_ref, out_ref,
                 halo_ref, x1_ref, h2_ref, f_ref):
    tm = x_ref.shape[0]

    @pl.when(pl.program_id(1) == 0)
    def _():
        halo_ref[...] = jnp.zeros_like(halo_ref)

    blocks = [slice(r0, r0 + ROWS_POST) for r0 in range(0, tm, ROWS_POST)]

    def up_proj(c, rows):
        hb = h2_ref[rows, :]
        return _dot(hb, wup_ref[c]), _dot(hb, wup_ref[c + N_FF_CHUNKS])

    mix = _dot(yr_ref[...], wo_ref[0:RET_WIDTH, :]) + _dot(ym_ref[...], wo_ref[RET_WIDTH:, :])
    x1 = x_ref[...] + _rms(mix, gpm_ref[...])
    x1_ref[...] = x1
    h2_ref[...] = _rms(x1, gpf_ref[...]).astype(BF16)
    e_in = _dot(p_ref[...].astype(BF16), wple_ref[...])

    nxt = [up_proj(0, rows) for rows in blocks]
    for c in range(N_FF_CHUNKS):
        cur = nxt
        nxt = []
        for b, rows in enumerate(blocks):
            if c + 1 < N_FF_CHUNKS:
                nxt.append(up_proj(c + 1, rows))
            ug, uu = cur[b]
            prev_g = halo_ref[c] if b == 0 else cur[b - 1][0][ROWS_POST - SUBLANES:]
            prev_u = halo_ref[c + N_FF_CHUNKS] if b == 0 else cur[b - 1][1][ROWS_POST - SUBLANES:]
            gate = _causal_conv(ug, cw_ref[c], prev_g)
            up = _causal_conv(uu, cw_ref[c + N_FF_CHUNKS], prev_u)
            act = (jax.nn.gelu(gate, approximate=True) * up).astype(BF16)
            dn = _dot(act, wdn_ref[c])
            if c == 0:
                f_ref[rows, :] = dn
            else:
                f_ref[rows, :] += dn
        halo_ref[c] = cur[-1][0][ROWS_POST - SUBLANES:]
        halo_ref[c + N_FF_CHUNKS] = cur[-1][1][ROWS_POST - SUBLANES:]

    x2 = x1_ref[...] + _rms(f_ref[...], gff_ref[...])
    ple_gate = jax.nn.sigmoid(_dot(x2.astype(BF16), wgate_ref[...]) + bgate_ref[...])
    out_ref[...] = x2 + _rms(e_in * ple_gate, gple_ref[...])


def _post(x2d, y_ret, y_mla, p2d, w_o, g_pm, g_pf, w_up_c, cw_c, w_dn_c, g_ff, w_ple, w_gate, b_gate,
          g_ple, batch, seq):
    t = x2d.shape[0]
    tm = TM_POST
    ns = seq // tm
    row = lambda b, s: (b * ns + s, 0)
    consts = (w_o, g_pm, g_pf, w_up_c, cw_c, w_dn_c, g_ff, w_ple, w_gate, b_gate, g_ple)
    return pl.pallas_call(
        _post_kernel,
        grid=(batch, ns),
        in_specs=[
            pl.BlockSpec((tm, D_MODEL), row),
            pl.BlockSpec((tm, RET_WIDTH), row),
            pl.BlockSpec((tm, MLA_WIDTH), row),
            pl.BlockSpec((tm, PLE_DIM), row),
        ] + [_const_spec(a.shape) for a in consts],
        out_specs=pl.BlockSpec((tm, D_MODEL), row),
        out_shape=jax.ShapeDtypeStruct((t, D_MODEL), F32),
        scratch_shapes=[
            pltpu.VMEM((2 * N_FF_CHUNKS, SUBLANES, FF_CHUNK), F32),
            pltpu.VMEM((tm, D_MODEL), F32),
            pltpu.VMEM((tm, D_MODEL), BF16),
            pltpu.VMEM((tm, D_MODEL), F32),
        ],
        compiler_params=pltpu.CompilerParams(
            dimension_semantics=("arbitrary", "arbitrary"), vmem_limit_bytes=VMEM_LIMIT),
        name="post",
    )(x2d, y_ret, y_mla, p2d, *consts)


def _layer(x2d, p2d, pos3, batch, seq, w_in, w_uq, w_ukv, w_o, g_pre_mix, g_post_mix, g_q_a, g_kv_a,
           g_ret_gn, w_up, conv_w, conv_b, w_down, g_pre_ffn, g_post_ffn, w_ple, w_ple_gate,
           b_ple_gate, g_post_ple):
    row = lambda v: v.reshape(1, -1)

    kpe_blk = jnp.pad(w_in[:, IN_USED:], ((0, 0), (MLA_NOPE_DIM, LANES - MLA_QK_DIM)))
    w_in_p = jnp.concatenate([w_in[:, RET_WIDTH:2 * RET_WIDTH], w_in[:, 3 * RET_WIDTH:IN_USED], kpe_blk],
                             axis=1).astype(BF16)
    w_q_t = w_in[:, :RET_WIDTH].T.astype(BF16)
    w_v_t = w_in[:, 2 * RET_WIDTH:3 * RET_WIDTH].T.astype(BF16)
    w_uq_t = jnp.pad(w_uq.reshape(MLA_Q_RANK, MLA_HEADS, MLA_QK_DIM),
                     ((0, 0), (0, 0), (0, LANES - MLA_QK_DIM))).reshape(MLA_Q_RANK, MLA_PAD).T.astype(BF16)
    w_ukv3 = w_ukv.reshape(MLA_KV_RANK, MLA_HEADS, MLA_NOPE_DIM + MLA_V_DIM)
    w_uk_p = jnp.pad(w_ukv3[..., :MLA_NOPE_DIM],
                     ((0, 0), (0, 0), (0, LANES - MLA_NOPE_DIM))).reshape(MLA_KV_RANK, MLA_PAD).astype(BF16)
    w_uv_t = w_ukv3[..., MLA_NOPE_DIM:].reshape(MLA_KV_RANK, MLA_WIDTH).T.astype(BF16)
    w_up_c = w_up.reshape(D_MODEL, 2 * N_FF_CHUNKS, FF_CHUNK).transpose(1, 0, 2).astype(BF16)
    w_dn_c = w_down.reshape(N_FF_CHUNKS, FF_CHUNK, D_MODEL).astype(BF16)
    cw4 = jnp.concatenate([conv_w, conv_b[None, :]], axis=0)
    cw_c = jnp.pad(cw4, ((0, SUBLANES - 4), (0, 0))).reshape(
        SUBLANES, 2 * N_FF_CHUNKS, FF_CHUNK).transpose(1, 0, 2)

    invf_r = ROPE_BASE ** (-jnp.arange(0, RET_HEAD_DIM, 2, dtype=F32) / RET_HEAD_DIM)
    invf_m = ROPE_BASE ** (-jnp.arange(0, MLA_ROPE_DIM, 2, dtype=F32) / MLA_ROPE_DIM)
    invf = jnp.concatenate([invf_r, invf_m]).reshape(-1, 1)
    log_gamma = jnp.log1p(-jnp.exp2(-5.0 - jnp.arange(RET_HEADS, dtype=F32)))

    k_r, g_r, k_m, q_rt, v_rt, q_t, v_t = _proj(
        pos3, x2d, invf, row(g_pre_mix), w_in_p, row(g_q_a), w_uq_t, row(g_kv_a), w_uk_p, w_uv_t, w_q_t, w_v_t)
    y_ret = _retention(log_gamma, q_rt, k_r, v_rt, g_r, row(g_ret_gn), batch, seq)
    y_mla = _mla(q_t, k_m, v_t, batch, seq)
    return _post(x2d, y_ret, y_mla, p2d, w_o.astype(BF16), row(g_post_mix), row(g_pre_ffn), w_up_c, cw_c,
                 w_dn_c, row(g_post_ffn), w_ple.astype(BF16), w_ple_gate.astype(BF16), row(b_ple_gate),
                 row(g_post_ple), batch, seq)


def kernel(x, p, positions, w_in, w_uq, w_ukv, w_o, g_pre_mix, g_post_mix, g_q_a, g_kv_a, g_ret_gn,
           w_up, conv_w, conv_b, w_down, g_pre_ffn, g_post_ffn, w_ple, w_ple_gate, b_ple_gate,
           g_post_ple):
    batch, seq, _ = x.shape
    depth = w_in.shape[0]
    assert seq % TM_PROJ == 0 and seq % TM_POST == 0 and seq % TQ_MLA == 0 and seq % RET_CHUNK == 0
    t = batch * seq
    x2d = x.reshape(t, D_MODEL)
    pos3 = positions.reshape(t // TM_PROJ, 1, TM_PROJ)
    for i in range(depth):
        x2d = _layer(x2d, p[i].reshape(t, PLE_DIM), pos3, batch, seq, w_in[i], w_uq[i], w_ukv[i], w_o[i],
                     g_pre_mix[i], g_post_mix[i], g_q_a[i], g_kv_a[i], g_ret_gn[i], w_up[i], conv_w[i],
                     conv_b[i], w_down[i], g_pre_ffn[i], g_post_ffn[i], w_ple[i], w_ple_gate[i],
                     b_ple_gate[i], g_post_ple[i])
    return x2d.reshape(batch, seq, D_MODEL)
```

```python
import jax
import jax.numpy as jnp
from jax import lax
from jax.experimental import pallas as pl
from jax.experimental.pallas import tpu as pltpu

F32 = jnp.float32
BF16 = jnp.bfloat16

D_MODEL = 1024
PLE_DIM = 256
RET_HEADS = 8
RET_HEAD_DIM = 64
RET_WIDTH = RET_HEADS * RET_HEAD_DIM
RET_CHUNK = 128
MLA_HEADS = 8
MLA_NOPE_DIM = 64
MLA_ROPE_DIM = 32
MLA_QK_DIM = MLA_NOPE_DIM + MLA_ROPE_DIM
MLA_V_DIM = 64
MLA_Q_RANK = 256
MLA_KV_RANK = 128
MLA_WIDTH = MLA_HEADS * MLA_V_DIM
D_FF = 2816
ROPE_BASE = 10000.0
EPS = 1e-6
LOG2_E = 1.4426950408889634

LANES = 128
SUBLANES = 8
IN_USED = 4 * RET_WIDTH + MLA_Q_RANK + MLA_KV_RANK
MLA_PAD = MLA_HEADS * LANES
N_FREQ_RET = RET_HEAD_DIM // 2
N_FREQ_MLA = MLA_ROPE_DIM // 2
FF_CHUNK = 256
N_FF_CHUNKS = D_FF // FF_CHUNK

TM_PROJ = 512
TM_POST = 512
ROWS_POST = 256
TQ_MLA = 256
COL_TILES_MLA = 1
ONES_ROWS = 16
VMEM_LIMIT = 56 * 1024 * 1024


def _dot(a, b):
    return jnp.dot(a, b, preferred_element_type=F32)


def _dot_nt(a, b):
    return lax.dot_general(a, b, (((1,), (1,)), ((), ())), preferred_element_type=F32)


def _rms(x, g):
    return x * lax.rsqrt(jnp.mean(x * x, axis=-1, keepdims=True) + EPS) * g


def _const_spec(shape):
    zeros = (0,) * len(shape)
    return pl.BlockSpec(shape, lambda *_: zeros, pipeline_mode=pl.Buffered(1))


def _proj_kernel(pos_ref, x_ref, invf_ref, gpre_ref, win_ref, gqa_ref, wuqt_ref, gkva_ref, wuk_ref, wuvt_ref,
                 wqt_ref, wvt_ref, kr_ref, gr_ref, km_ref, qrt_ref, vrt_ref, qt_ref, vt_ref):
    tm = x_ref.shape[0]
    h_f32 = _rms(x_ref[...], gpre_ref[...])
    h = h_f32.astype(BF16)

    zc = _dot(h, win_ref[:, 2 * RET_WIDTH:])
    zk = _dot(h, win_ref[:, 0:RET_WIDTH])
    zg = _dot(h, win_ref[:, RET_WIDTH:2 * RET_WIDTH])

    ang = invf_ref[...] * pos_ref[0].astype(F32)
    cos_t = jnp.cos(ang)
    sin_t = jnp.sin(ang)
    cos_rt, sin_rt = cos_t[:N_FREQ_RET], sin_t[:N_FREQ_RET]
    cos_mt, sin_mt = cos_t[N_FREQ_RET:], sin_t[N_FREQ_RET:]

    h_t = h_f32.T.astype(BF16)
    q_rt = _dot(wqt_ref[...], h_t)
    v_rt = _dot(wvt_ref[...], h_t)

    cq = _rms(zc[:, :MLA_Q_RANK], gqa_ref[...])
    ckv = _rms(zc[:, MLA_Q_RANK:MLA_Q_RANK + MLA_KV_RANK], gkva_ref[...])
    q_t = _dot(wuqt_ref[...], cq.T.astype(BF16))
    k_up = _dot(ckv.astype(BF16), wuk_ref[...])
    v_t = _dot(wuvt_ref[...], ckv.T.astype(BF16))

    gr_ref[...] = jax.nn.silu(zg).astype(BF16)

    reps_r = LANES // N_FREQ_RET
    reps_m = LANES // N_FREQ_MLA
    cos_r = jnp.concatenate([cos_rt] * reps_r, axis=0).T
    sin_r = jnp.concatenate([sin_rt] * reps_r, axis=0).T
    lane = lax.broadcasted_iota(jnp.int32, (1, LANES), 1)
    first_half = (lane & (RET_HEAD_DIM - 1)) < N_FREQ_RET
    sin_r_lo = jnp.where(first_half, -sin_r, 0.0)
    sin_r_hi = jnp.where(first_half, 0.0, sin_r)
    k_scale = RET_HEAD_DIM ** -0.5
    for c in range(RET_WIDTH // LANES):
        sl = slice(c * LANES, (c + 1) * LANES)
        blk = zk[:, sl]
        roped = (blk * cos_r + pltpu.roll(blk, LANES - N_FREQ_RET, 1) * sin_r_lo
                 + pltpu.roll(blk, N_FREQ_RET, 1) * sin_r_hi)
        kr_ref[:, sl] = (roped * k_scale).astype(BF16)

    for hh in range(RET_HEADS):
        b0 = hh * RET_HEAD_DIM
        x1 = q_rt[b0:b0 + N_FREQ_RET]
        x2 = q_rt[b0 + N_FREQ_RET:b0 + RET_HEAD_DIM]
        qrt_ref[b0:b0 + RET_HEAD_DIM, :] = jnp.concatenate(
            [x1 * cos_rt - x2 * sin_rt, x2 * cos_rt + x1 * sin_rt], axis=0).astype(BF16)
    vrt_ref[...] = v_rt.astype(BF16)

    r0 = MLA_NOPE_DIM
    r1 = MLA_NOPE_DIM + N_FREQ_MLA
    r2 = MLA_NOPE_DIM + MLA_ROPE_DIM
    cos_m = jnp.concatenate([cos_mt] * reps_m, axis=0).T
    sin_m = jnp.concatenate([sin_mt] * reps_m, axis=0).T
    in_rot = (lane >= r0) & (lane < r2)
    cos_m_f = jnp.where(in_rot, cos_m, 0.0)
    sin_m_lo = jnp.where((lane >= r0) & (lane < r1), -sin_m, 0.0)
    sin_m_hi = jnp.where((lane >= r1) & (lane < r2), sin_m, 0.0)
    kpe = zc[:, MLA_Q_RANK + MLA_KV_RANK:]
    k_rot = (kpe * cos_m_f + pltpu.roll(kpe, LANES - N_FREQ_MLA, 1) * sin_m_lo
             + pltpu.roll(kpe, N_FREQ_MLA, 1) * sin_m_hi)
    for hh in range(MLA_HEADS):
        sl = slice(hh * LANES, (hh + 1) * LANES)
        km_ref[:, sl] = (k_up[:, sl] + k_rot).astype(BF16)
    vt_ref[...] = v_t.astype(BF16)

    q_scale = MLA_QK_DIM ** -0.5 * LOG2_E
    pad = jnp.zeros((LANES - r2, tm), F32)
    for hh in range(MLA_HEADS):
        b0 = hh * LANES
        x1 = q_t[b0 + r0:b0 + r1]
        x2 = q_t[b0 + r1:b0 + r2]
        blk = jnp.concatenate([q_t[b0:b0 + r0], x1 * cos_mt - x2 * sin_mt, x2 * cos_mt + x1 * sin_mt, pad],
                              axis=0)
        qt_ref[b0:b0 + LANES, :] = (blk * q_scale).astype(BF16)


def _proj(pos3, x2d, invf, g_pre, w_in_p, g_qa, w_uq_t, g_kva, w_uk_p, w_uv_t, w_q_t, w_v_t):
    t = x2d.shape[0]
    tm = TM_PROJ
    row = lambda i: (i, 0)
    col = lambda i: (0, i)
    row_widths = (RET_WIDTH, RET_WIDTH, MLA_PAD)
    col_heights = (RET_WIDTH, RET_WIDTH, MLA_PAD, MLA_WIDTH)
    return pl.pallas_call(
        _proj_kernel,
        grid=(t // tm,),
        in_specs=[
            pl.BlockSpec((1, 1, tm), lambda i: (i, 0, 0)),
            pl.BlockSpec((tm, D_MODEL), row),
            _const_spec(invf.shape),
            _const_spec(g_pre.shape),
            _const_spec(w_in_p.shape),
            _const_spec(g_qa.shape),
            _const_spec(w_uq_t.shape),
            _const_spec(g_kva.shape),
            _const_spec(w_uk_p.shape),
            _const_spec(w_uv_t.shape),
            _const_spec(w_q_t.shape),
            _const_spec(w_v_t.shape),
        ],
        out_specs=([pl.BlockSpec((tm, w), row) for w in row_widths]
                   + [pl.BlockSpec((hgt, tm), col) for hgt in col_heights]),
        out_shape=([jax.ShapeDtypeStruct((t, w), BF16) for w in row_widths]
                   + [jax.ShapeDtypeStruct((hgt, t), BF16) for hgt in col_heights]),
        compiler_params=pltpu.CompilerParams(
            dimension_semantics=("parallel",), vmem_limit_bytes=VMEM_LIMIT),
        name="proj",
    )(pos3, x2d, invf, g_pre, w_in_p, g_qa, w_uq_t, g_kva, w_uk_p, w_uv_t, w_q_t, w_v_t)


def _ret_kernel(lg_ref, qt_ref, k_ref, vt_ref, g_ref, gn_ref, o_ref):
    hp = pl.program_id(1)
    c = RET_CHUNK
    d = RET_HEAD_DIM
    n = k_ref.shape[0] // c
    lg0 = lg_ref[2 * hp]
    lg1 = lg_ref[2 * hp + 1]
    lane = lax.broadcasted_iota(jnp.int32, (1, LANES), 1)
    sub = lax.broadcasted_iota(jnp.int32, (LANES, 1), 0)
    lg_lane = jnp.where(lane < d, lg0, lg1)
    lg_sub = jnp.where(sub < d, lg0, lg1)
    key = lax.broadcasted_iota(jnp.int32, (c, c), 0)
    qry = lax.broadcasted_iota(jnp.int32, (c, c), 1)
    rel = (qry - key).astype(F32)
    causal = rel >= 0
    relp = jnp.maximum(rel, 0.0)
    intra = (jnp.where(causal, jnp.exp(lg0 * relp), 0.0), jnp.where(causal, jnp.exp(lg1 * relp), 0.0))
    tok_lane = lax.broadcasted_iota(jnp.int32, (1, c), 1).astype(F32)
    tok_sub = lax.broadcasted_iota(jnp.int32, (c, 1), 0).astype(F32)
    inner_t = jnp.exp(lg_sub * (tok_lane + 1.0))
    tail = jnp.exp(lg_lane * (c - 1.0 - tok_sub))
    decay = jnp.exp(lg_sub * float(c))
    same_head = (key < d) == (qry < d)
    zeros = jnp.zeros((d, c), BF16)
    heads = (0, 1)
    chunks = range(n)

    def tok(i):
        return slice(i * c, (i + 1) * c)

    q_t = [qt_ref[:, tok(i)] for i in chunks]
    q_h = [(jnp.concatenate([q_t[i][:d], zeros], axis=0), jnp.concatenate([zeros, q_t[i][d:]], axis=0))
           for i in chunks]
    s_t = [[(_dot(k_ref[tok(i), :], q_h[i][hh]) * intra[hh]).astype(BF16) for hh in heads] for i in chunks]
    w_t = [jnp.concatenate([_dot(vt_ref[hh * d:(hh + 1) * d, tok(i)], s_t[i][hh]) for hh in heads], axis=0)
           for i in chunks]
    upd = [jnp.where(same_head,
                     _dot(vt_ref[:, tok(i)], (k_ref[tok(i), :].astype(F32) * tail).astype(BF16)), 0.0)
           for i in chunks]
    states = [jnp.zeros((LANES, LANES), F32)]
    for i in range(n - 1):
        states.append(states[i] * decay + upd[i])
    gn = gn_ref[...]
    for i in chunks:
        o_t = w_t[i] if i == 0 else _dot(states[i].astype(BF16), q_t[i]) * inner_t + w_t[i]
        normed = []
        for hh in heads:
            o_h = o_t[hh * d:(hh + 1) * d]
            dlt = o_h - jnp.mean(o_h, axis=0, keepdims=True)
            var = jnp.mean(dlt * dlt, axis=0, keepdims=True)
            normed.append(dlt * lax.rsqrt(var + EPS))
        on = jnp.concatenate(normed, axis=0).T
        o_ref[tok(i), :] = (g_ref[tok(i), :].astype(F32) * (on * gn)).astype(BF16)


def _retention(log_gamma, q_rt, k_r, v_rt, g_r, g_gn, batch, seq):
    t = k_r.shape[0]
    row_blk = pl.BlockSpec((seq, LANES), lambda b, hp: (b, hp))
    col_blk = pl.BlockSpec((LANES, seq), lambda b, hp: (hp, b))
    return pl.pallas_call(
        _ret_kernel,
        grid=(batch, RET_WIDTH // LANES),
        in_specs=[
            pl.BlockSpec(memory_space=pltpu.SMEM),
            col_blk, row_blk, col_blk, row_blk,
            pl.BlockSpec((1, LANES), lambda b, hp: (0, hp)),
        ],
        out_specs=row_blk,
        out_shape=jax.ShapeDtypeStruct((t, RET_WIDTH), BF16),
        compiler_params=pltpu.CompilerParams(
            dimension_semantics=("parallel", "parallel"), vmem_limit_bytes=VMEM_LIMIT),
        name="retention",
    )(log_gamma, q_rt, k_r, v_rt, g_r, g_gn)


def _mla_kernel(qt_ref, k_ref, vt_ref, o_ref, m_ref, acc_ref):
    tile = TQ_MLA
    n_tiles = k_ref.shape[0] // tile
    key = lax.broadcasted_iota(jnp.int32, (tile, tile), 0)
    qry = lax.broadcasted_iota(jnp.int32, (tile, tile), 1)
    diag_ok = key <= qry
    heads = (0, 1)
    dv = MLA_V_DIM
    ones = jnp.ones((ONES_ROWS, tile), BF16)

    def scores(d, hh):
        return _dot(k_ref[d * tile:(d + 1) * tile, hh * LANES:(hh + 1) * LANES],
                    qt_ref[hh * LANES:(hh + 1) * LANES, d * tile:])

    def update(d, hh, s, cols):
        v_ext = jnp.concatenate([vt_ref[hh * dv:(hh + 1) * dv, d * tile:(d + 1) * tile], ones], axis=0)
        m_t = jnp.max(s, axis=0, keepdims=True)
        if d == 0:
            m_ref[hh, :, cols] = m_t
            acc_ref[hh, :, cols] = _dot(v_ext, jnp.exp2(s - m_t).astype(BF16))
        else:
            m = m_ref[hh, :, cols]
            m_new = jnp.maximum(m, m_t)
            m_ref[hh, :, cols] = m_new
            pv = _dot(v_ext, jnp.exp2(s - m_new).astype(BF16))
            acc_ref[hh, :, cols] = jnp.exp2(m - m_new) * acc_ref[hh, :, cols] + pv

    s_next = {hh: scores(0, hh) for hh in heads}
    for d in range(n_tiles):
        done = slice(d * tile, (d + 1) * tile)
        for hh in heads:
            s = s_next[hh]
            if d + 1 < n_tiles:
                s_next[hh] = scores(d + 1, hh)
            update(d, hh, jnp.where(diag_ok, s[:, :tile], -jnp.inf), done)
            for c0 in range(d + 1, n_tiles, COL_TILES_MLA):
                c1 = min(c0 + COL_TILES_MLA, n_tiles)
                update(d, hh, s[:, (c0 - d) * tile:(c1 - d) * tile], slice(c0 * tile, c1 * tile))
        o_t = jnp.concatenate([acc_ref[hh, :dv, done] / acc_ref[hh, dv:dv + 1, done] for hh in heads], axis=0)
        o_ref[done, :] = o_t.T.astype(BF16)


def _mla(q_t, k_m, v_t, batch, seq):
    t = k_m.shape[0]
    return pl.pallas_call(
        _mla_kernel,
        grid=(batch, MLA_WIDTH // LANES),
        in_specs=[
            pl.BlockSpec((2 * LANES, seq), lambda b, hp: (hp, b)),
            pl.BlockSpec((seq, 2 * LANES), lambda b, hp: (b, hp)),
            pl.BlockSpec((LANES, seq), lambda b, hp: (hp, b)),
        ],
        out_specs=pl.BlockSpec((seq, LANES), lambda b, hp: (b, hp)),
        out_shape=jax.ShapeDtypeStruct((t, MLA_WIDTH), BF16),
        scratch_shapes=[
            pltpu.VMEM((2, 1, seq), F32),
            pltpu.VMEM((2, MLA_V_DIM + ONES_ROWS, seq), F32),
        ],
        compiler_params=pltpu.CompilerParams(
            dimension_semantics=("parallel", "parallel"), vmem_limit_bytes=VMEM_LIMIT),
        name="mla_attention",
    )(q_t, k_m, v_t)


def _causal_conv(u, cw, halo):
    row = lax.broadcasted_iota(jnp.int32, halo.shape, 0)

    def shifted(n):
        r = pltpu.roll(u, n, 0)
        first = jnp.where(row < n, pltpu.roll(halo, n, 0), r[:SUBLANES])
        return jnp.concatenate([first, r[SUBLANES:]], axis=0)

    return cw[0:1] * shifted(2) + cw[1:2] * shifted(1) + cw[2:3] * u + cw[3:4]


def _post_kernel(x_ref, yr_ref, ym_ref, p_ref, wo_ref, gpm_ref, gpf_ref, wup_ref, cw_ref, wdn_ref,
                 gff_ref, wple_ref, wgate_ref, bgate_ref, gple_ref, out_ref,
                 halo_ref, x1_ref, h2_ref, f_ref):
    tm = x_ref.shape[0]

    @pl.when(pl.program_id(1) == 0)
    def _():
        halo_ref[...] = jnp.zeros_like(halo_ref)

    n_blk = tm // ROWS_POST
    n_units = N_FF_CHUNKS * n_blk
    half = D_MODEL // 2

    def rows_of(u):
        r0 = (u % n_blk) * ROWS_POST
        return slice(r0, r0 + ROWS_POST)

    mix = _dot(yr_ref[...], wo_ref[0:RET_WIDTH, :]) + _dot(ym_ref[...], wo_ref[RET_WIDTH:, :])
    x1 = x_ref[...] + _rms(mix, gpm_ref[...])
    x1_ref[...] = x1
    h2_ref[...] = _rms(x1, gpf_ref[...]).astype(BF16)
    e_in = _dot(p_ref[...].astype(BF16), wple_ref[...])

    def up_gate(u):
        return _dot(h2_ref[rows_of(u), :], wup_ref[u // n_blk])

    def up_val(u):
        return _dot(h2_ref[rows_of(u), :], wup_ref[u // n_blk + N_FF_CHUNKS])

    def geglu(u, ug, uu, prev, q):
        c, b = divmod(u, n_blk)
        qr = ROWS_POST // 4
        r0 = q * qr
        if r0 > 0:
            halo_g, halo_u = ug[r0 - SUBLANES:r0], uu[r0 - SUBLANES:r0]
        elif b > 0:
            halo_g, halo_u = prev[0][ROWS_POST - SUBLANES:], prev[1][ROWS_POST - SUBLANES:]
        else:
            halo_g, halo_u = halo_ref[c], halo_ref[c + N_FF_CHUNKS]
        gate = _causal_conv(ug[r0:r0 + qr], cw_ref[c], halo_g)
        up = _causal_conv(uu[r0:r0 + qr], cw_ref[c + N_FF_CHUNKS], halo_u)
        return (jax.nn.gelu(gate, approximate=True) * up).astype(BF16)

    def down(u, act, n):
        c = u // n_blk
        cols = slice(n * half, (n + 1) * half)
        dn = _dot_nt(wdn_ref[c, cols, :], act)
        if c == 0:
            f_ref[cols, rows_of(u)] = dn
        else:
            f_ref[cols, rows_of(u)] += dn

    cur = (up_gate(0), up_val(0))
    prev = None
    act_prev = None
    for u in range(n_units):
        more = u + 1 < n_units
        ug, uu = cur
        nxt_g = up_gate(u + 1) if more else None
        a0 = geglu(u, ug, uu, prev, 0)
        nxt_u = up_val(u + 1) if more else None
        a1 = geglu(u, ug, uu, prev, 1)
        if act_prev is not None:
            down(u - 1, act_prev, 0)
        a2 = geglu(u, ug, uu, prev, 2)
        if act_prev is not None:
            down(u - 1, act_prev, 1)
        a3 = geglu(u, ug, uu, prev, 3)
        act_prev = jnp.concatenate([a0, a1, a2, a3], axis=0)
        if u % n_blk == n_blk - 1:
            c = u // n_blk
            halo_ref[c] = ug[ROWS_POST - SUBLANES:]
            halo_ref[c + N_FF_CHUNKS] = uu[ROWS_POST - SUBLANES:]
        prev = cur
        cur = (nxt_g, nxt_u)
    down(n_units - 1, act_prev, 0)
    down(n_units - 1, act_prev, 1)

    for b in range(n_blk):
        rows = rows_of(b)
        x2 = x1_ref[rows, :] + _rms(f_ref[:, rows].T, gff_ref[...])
        ple_gate = jax.nn.sigmoid(_dot(x2.astype(BF16), wgate_ref[...]) + bgate_ref[...])
        out_ref[rows, :] = x2 + _rms(e_in[rows] * ple_gate, gple_ref[...])


def _post(x2d, y_ret, y_mla, p2d, w_o, g_pm, g_pf, w_up_c, cw_c, w_dn_c, g_ff, w_ple, w_gate, b_gate,
          g_ple, batch, seq):
    t = x2d.shape[0]
    tm = TM_POST
    ns = seq // tm
    row = lambda b, s: (b * ns + s, 0)
    consts = (w_o, g_pm, g_pf, w_up_c, cw_c, w_dn_c, g_ff, w_ple, w_gate, b_gate, g_ple)
    return pl.pallas_call(
        _post_kernel,
        grid=(batch, ns),
        in_specs=[
            pl.BlockSpec((tm, D_MODEL), row),
            pl.BlockSpec((tm, RET_WIDTH), row),
            pl.BlockSpec((tm, MLA_WIDTH), row),
            pl.BlockSpec((tm, PLE_DIM), row),
        ] + [_const_spec(a.shape) for a in consts],
        out_specs=pl.BlockSpec((tm, D_MODEL), row),
        out_shape=jax.ShapeDtypeStruct((t, D_MODEL), F32),
        scratch_shapes=[
            pltpu.VMEM((2 * N_FF_CHUNKS, SUBLANES, FF_CHUNK), F32),
            pltpu.VMEM((tm, D_MODEL), F32),
            pltpu.VMEM((tm, D_MODEL), BF16),
            pltpu.VMEM((D_MODEL, tm), F32),
        ],
        compiler_params=pltpu.CompilerParams(
            dimension_semantics=("arbitrary", "arbitrary"), vmem_limit_bytes=VMEM_LIMIT),
        name="post",
    )(x2d, y_ret, y_mla, p2d, *consts)


def _layer(x2d, p2d, pos3, batch, seq, w_in, w_uq, w_ukv, w_o, g_pre_mix, g_post_mix, g_q_a, g_kv_a,
           g_ret_gn, w_up, conv_w, conv_b, w_down, g_pre_ffn, g_post_ffn, w_ple, w_ple_gate,
           b_ple_gate, g_post_ple):
    row = lambda v: v.reshape(1, -1)

    kpe_blk = jnp.pad(w_in[:, IN_USED:], ((0, 0), (MLA_NOPE_DIM, LANES - MLA_QK_DIM)))
    w_in_p = jnp.concatenate([w_in[:, RET_WIDTH:2 * RET_WIDTH], w_in[:, 3 * RET_WIDTH:IN_USED], kpe_blk],
                             axis=1).astype(BF16)
    w_q_t = w_in[:, :RET_WIDTH].T.astype(BF16)
    w_v_t = w_in[:, 2 * RET_WIDTH:3 * RET_WIDTH].T.astype(BF16)
    w_uq_t = jnp.pad(w_uq.reshape(MLA_Q_RANK, MLA_HEADS, MLA_QK_DIM),
                     ((0, 0), (0, 0), (0, LANES - MLA_QK_DIM))).reshape(MLA_Q_RANK, MLA_PAD).T.astype(BF16)
    w_ukv3 = w_ukv.reshape(MLA_KV_RANK, MLA_HEADS, MLA_NOPE_DIM + MLA_V_DIM)
    w_uk_p = jnp.pad(w_ukv3[..., :MLA_NOPE_DIM],
                     ((0, 0), (0, 0), (0, LANES - MLA_NOPE_DIM))).reshape(MLA_KV_RANK, MLA_PAD).astype(BF16)
    w_uv_t = w_ukv3[..., MLA_NOPE_DIM:].reshape(MLA_KV_RANK, MLA_WIDTH).T.astype(BF16)
    w_up_c = w_up.reshape(D_MODEL, 2 * N_FF_CHUNKS, FF_CHUNK).transpose(1, 0, 2).astype(BF16)
    w_dn_c = w_down.reshape(N_FF_CHUNKS, FF_CHUNK, D_MODEL).transpose(0, 2, 1).astype(BF16)
    cw4 = jnp.concatenate([conv_w, conv_b[None, :]], axis=0)
    cw_c = jnp.pad(cw4, ((0, SUBLANES - 4), (0, 0))).reshape(
        SUBLANES, 2 * N_FF_CHUNKS, FF_CHUNK).transpose(1, 0, 2)

    invf_r = ROPE_BASE ** (-jnp.arange(0, RET_HEAD_DIM, 2, dtype=F32) / RET_HEAD_DIM)
    invf_m = ROPE_BASE ** (-jnp.arange(0, MLA_ROPE_DIM, 2, dtype=F32) / MLA_ROPE_DIM)
    invf = jnp.concatenate([invf_r, invf_m]).reshape(-1, 1)
    log_gamma = jnp.log1p(-jnp.exp2(-5.0 - jnp.arange(RET_HEADS, dtype=F32)))

    k_r, g_r, k_m, q_rt, v_rt, q_t, v_t = _proj(
        pos3, x2d, invf, row(g_pre_mix), w_in_p, row(g_q_a), w_uq_t, row(g_kv_a), w_uk_p, w_uv_t, w_q_t, w_v_t)
    y_ret = _retention(log_gamma, q_rt, k_r, v_rt, g_r, row(g_ret_gn), batch, seq)
    y_mla = _mla(q_t, k_m, v_t, batch, seq)
    return _post(x2d, y_ret, y_mla, p2d, w_o.astype(BF16), row(g_post_mix), row(g_pre_ffn), w_up_c, cw_c,
                 w_dn_c, row(g_post_ffn), w_ple.astype(BF16), w_ple_gate.astype(BF16), row(b_ple_gate),
                 row(g_post_ple), batch, seq)


def kernel(x, p, positions, w_in, w_uq, w_ukv, w_o, g_pre_mix, g_post_mix, g_q_a, g_kv_a, g_ret_gn,
           w_up, conv_w, conv_b, w_down, g_pre_ffn, g_post_ffn, w_ple, w_ple_gate, b_ple_gate,
           g_post_ple):
    batch, seq, _ = x.shape
    depth = w_in.shape[0]
    assert seq % TM_PROJ == 0 and seq % TM_POST == 0 and seq % TQ_MLA == 0 and seq % RET_CHUNK == 0
    t = batch * seq
    x2d = x.reshape(t, D_MODEL)
    pos3 = positions.reshape(t // TM_PROJ, 1, TM_PROJ)
    for i in range(depth):
        x2d = _layer(x2d, p[i].reshape(t, PLE_DIM), pos3, batch, seq, w_in[i], w_uq[i], w_ukv[i], w_o[i],
                     g_pre_mix[i], g_post_mix[i], g_q_a[i], g_kv_a[i], g_ret_gn[i], w_up[i], conv_w[i],
                     conv_b[i], w_down[i], g_pre_ffn[i], g_post_ffn[i], w_ple[i], w_ple_gate[i],
                     b_ple_gate[i], g_post_ple[i])
    return x2d.reshape(batch, seq, D_MODEL)
```

```python
import jax
import jax.numpy as jnp
from jax import lax
from jax.experimental import pallas as pl
from jax.experimental.pallas import tpu as pltpu

F32 = jnp.float32
BF16 = jnp.bfloat16

D_MODEL = 1024
PLE_DIM = 256
RET_HEADS = 8
RET_HEAD_DIM = 64
RET_WIDTH = RET_HEADS * RET_HEAD_DIM
RET_CHUNK = 128
MLA_HEADS = 8
MLA_NOPE_DIM = 64
MLA_ROPE_DIM = 32
MLA_QK_DIM = MLA_NOPE_DIM + MLA_ROPE_DIM
MLA_V_DIM = 64
MLA_Q_RANK = 256
MLA_KV_RANK = 128
MLA_WIDTH = MLA_HEADS * MLA_V_DIM
D_FF = 2816
ROPE_BASE = 10000.0
EPS = 1e-6
LOG2_E = 1.4426950408889634
GELU_C1 = 0.7978845608028654
GELU_C2 = GELU_C1 * 0.044715

LANES = 128
SUBLANES = 8
IN_USED = 4 * RET_WIDTH + MLA_Q_RANK + MLA_KV_RANK
MLA_PAD = MLA_HEADS * LANES
N_FREQ_RET = RET_HEAD_DIM // 2
N_FREQ_MLA = MLA_ROPE_DIM // 2
FF_CHUNK = 256
N_FF_CHUNKS = D_FF // FF_CHUNK

TM_PROJ = 512
TM_POST = 512
ROWS_POST = 256
TQ_MLA = 256
COL_TILES_MLA = 1
ONES_ROWS = 16
VMEM_LIMIT = 56 * 1024 * 1024


def _dot(a, b):
    return jnp.dot(a, b, preferred_element_type=F32)


def _rms(x, g):
    return x * lax.rsqrt(jnp.mean(x * x, axis=-1, keepdims=True) + EPS) * g


def _const_spec(shape):
    zeros = (0,) * len(shape)
    return pl.BlockSpec(shape, lambda *_: zeros, pipeline_mode=pl.Buffered(1))


def _proj_kernel(pos_ref, x_ref, invf_ref, gpre_ref, win_ref, gqa_ref, wuqt_ref, gkva_ref, wuk_ref, wuvt_ref,
                 wqt_ref, wvt_ref, kr_ref, gr_ref, km_ref, qrt_ref, vrt_ref, qt_ref, vt_ref):
    tm = x_ref.shape[0]
    h_f32 = _rms(x_ref[...], gpre_ref[...])
    h = h_f32.astype(BF16)

    zc = _dot(h, win_ref[:, 2 * RET_WIDTH:])
    zk = _dot(h, win_ref[:, 0:RET_WIDTH])
    zg = _dot(h, win_ref[:, RET_WIDTH:2 * RET_WIDTH])

    ang = invf_ref[...] * pos_ref[0].astype(F32)
    cos_t = jnp.cos(ang)
    sin_t = jnp.sin(ang)
    cos_rt, sin_rt = cos_t[:N_FREQ_RET], sin_t[:N_FREQ_RET]
    cos_mt, sin_mt = cos_t[N_FREQ_RET:], sin_t[N_FREQ_RET:]

    h_t = h_f32.T.astype(BF16)
    q_rt = _dot(wqt_ref[...], h_t)
    v_rt = _dot(wvt_ref[...], h_t)

    cq = _rms(zc[:, :MLA_Q_RANK], gqa_ref[...])
    ckv = _rms(zc[:, MLA_Q_RANK:MLA_Q_RANK + MLA_KV_RANK], gkva_ref[...])
    q_t = _dot(wuqt_ref[...], cq.T.astype(BF16))
    k_up = _dot(ckv.astype(BF16), wuk_ref[...])
    v_t = _dot(wuvt_ref[...], ckv.T.astype(BF16))

    gr_ref[...] = jax.nn.silu(zg).astype(BF16)

    reps_r = LANES // N_FREQ_RET
    reps_m = LANES // N_FREQ_MLA
    cos_r = jnp.concatenate([cos_rt] * reps_r, axis=0).T
    sin_r = jnp.concatenate([sin_rt] * reps_r, axis=0).T
    lane = lax.broadcasted_iota(jnp.int32, (1, LANES), 1)
    first_half = (lane & (RET_HEAD_DIM - 1)) < N_FREQ_RET
    sin_r_lo = jnp.where(first_half, -sin_r, 0.0)
    sin_r_hi = jnp.where(first_half, 0.0, sin_r)
    k_scale = RET_HEAD_DIM ** -0.5
    for c in range(RET_WIDTH // LANES):
        sl = slice(c * LANES, (c + 1) * LANES)
        blk = zk[:, sl]
        roped = (blk * cos_r + pltpu.roll(blk, LANES - N_FREQ_RET, 1) * sin_r_lo
                 + pltpu.roll(blk, N_FREQ_RET, 1) * sin_r_hi)
        kr_ref[:, sl] = (roped * k_scale).astype(BF16)

    for hh in range(RET_HEADS):
        b0 = hh * RET_HEAD_DIM
        x1 = q_rt[b0:b0 + N_FREQ_RET]
        x2 = q_rt[b0 + N_FREQ_RET:b0 + RET_HEAD_DIM]
        qrt_ref[b0:b0 + RET_HEAD_DIM, :] = jnp.concatenate(
            [x1 * cos_rt - x2 * sin_rt, x2 * cos_rt + x1 * sin_rt], axis=0).astype(BF16)
    vrt_ref[...] = v_rt.astype(BF16)

    r0 = MLA_NOPE_DIM
    r1 = MLA_NOPE_DIM + N_FREQ_MLA
    r2 = MLA_NOPE_DIM + MLA_ROPE_DIM
    cos_m = jnp.concatenate([cos_mt] * reps_m, axis=0).T
    sin_m = jnp.concatenate([sin_mt] * reps_m, axis=0).T
    in_rot = (lane >= r0) & (lane < r2)
    cos_m_f = jnp.where(in_rot, cos_m, 0.0)
    sin_m_lo = jnp.where((lane >= r0) & (lane < r1), -sin_m, 0.0)
    sin_m_hi = jnp.where((lane >= r1) & (lane < r2), sin_m, 0.0)
    kpe = zc[:, MLA_Q_RANK + MLA_KV_RANK:]
    k_rot = (kpe * cos_m_f + pltpu.roll(kpe, LANES - N_FREQ_MLA, 1) * sin_m_lo
             + pltpu.roll(kpe, N_FREQ_MLA, 1) * sin_m_hi)
    for hh in range(MLA_HEADS):
        sl = slice(hh * LANES, (hh + 1) * LANES)
        km_ref[:, sl] = (k_up[:, sl] + k_rot).astype(BF16)
    vt_ref[...] = v_t.astype(BF16)

    q_scale = MLA_QK_DIM ** -0.5 * LOG2_E
    pad = jnp.zeros((LANES - r2, tm), F32)
    for hh in range(MLA_HEADS):
        b0 = hh * LANES
        x1 = q_t[b0 + r0:b0 + r1]
        x2 = q_t[b0 + r1:b0 + r2]
        blk = jnp.concatenate([q_t[b0:b0 + r0], x1 * cos_mt - x2 * sin_mt, x2 * cos_mt + x1 * sin_mt, pad],
                              axis=0)
        qt_ref[b0:b0 + LANES, :] = (blk * q_scale).astype(BF16)


def _proj(pos3, x2d, invf, g_pre, w_in_p, g_qa, w_uq_t, g_kva, w_uk_p, w_uv_t, w_q_t, w_v_t):
    t = x2d.shape[0]
    tm = TM_PROJ
    row = lambda i: (i, 0)
    col = lambda i: (0, i)
    row_widths = (RET_WIDTH, RET_WIDTH, MLA_PAD)
    col_heights = (RET_WIDTH, RET_WIDTH, MLA_PAD, MLA_WIDTH)
    return pl.pallas_call(
        _proj_kernel,
        grid=(t // tm,),
        in_specs=[
            pl.BlockSpec((1, 1, tm), lambda i: (i, 0, 0)),
            pl.BlockSpec((tm, D_MODEL), row),
            _const_spec(invf.shape),
            _const_spec(g_pre.shape),
            _const_spec(w_in_p.shape),
            _const_spec(g_qa.shape),
            _const_spec(w_uq_t.shape),
            _const_spec(g_kva.shape),
            _const_spec(w_uk_p.shape),
            _const_spec(w_uv_t.shape),
            _const_spec(w_q_t.shape),
            _const_spec(w_v_t.shape),
        ],
        out_specs=([pl.BlockSpec((tm, w), row) for w in row_widths]
                   + [pl.BlockSpec((hgt, tm), col) for hgt in col_heights]),
        out_shape=([jax.ShapeDtypeStruct((t, w), BF16) for w in row_widths]
                   + [jax.ShapeDtypeStruct((hgt, t), BF16) for hgt in col_heights]),
        compiler_params=pltpu.CompilerParams(
            dimension_semantics=("parallel",), vmem_limit_bytes=VMEM_LIMIT),
        name="proj",
    )(pos3, x2d, invf, g_pre, w_in_p, g_qa, w_uq_t, g_kva, w_uk_p, w_uv_t, w_q_t, w_v_t)


def _ret_kernel(lg_ref, qt_ref, k_ref, vt_ref, g_ref, gn_ref, o_ref):
    hp = pl.program_id(1)
    c = RET_CHUNK
    d = RET_HEAD_DIM
    n = k_ref.shape[0] // c
    lg0 = lg_ref[2 * hp]
    lg1 = lg_ref[2 * hp + 1]
    lane = lax.broadcasted_iota(jnp.int32, (1, LANES), 1)
    sub = lax.broadcasted_iota(jnp.int32, (LANES, 1), 0)
    lg_lane = jnp.where(lane < d, lg0, lg1)
    lg_sub = jnp.where(sub < d, lg0, lg1)
    key = lax.broadcasted_iota(jnp.int32, (c, c), 0)
    qry = lax.broadcasted_iota(jnp.int32, (c, c), 1)
    rel = (qry - key).astype(F32)
    causal = rel >= 0
    relp = jnp.maximum(rel, 0.0)
    intra = (jnp.where(causal, jnp.exp(lg0 * relp), 0.0), jnp.where(causal, jnp.exp(lg1 * relp), 0.0))
    tok_lane = lax.broadcasted_iota(jnp.int32, (1, c), 1).astype(F32)
    tok_sub = lax.broadcasted_iota(jnp.int32, (c, 1), 0).astype(F32)
    inner_t = jnp.exp(lg_sub * (tok_lane + 1.0))
    tail = jnp.exp(lg_lane * (c - 1.0 - tok_sub))
    decay = jnp.exp(lg_sub * float(c))
    same_head = (key < d) == (qry < d)
    zeros = jnp.zeros((d, c), BF16)
    heads = (0, 1)
    chunks = range(n)

    def tok(i):
        return slice(i * c, (i + 1) * c)

    q_t = [qt_ref[:, tok(i)] for i in chunks]
    q_h = [(jnp.concatenate([q_t[i][:d], zeros], axis=0), jnp.concatenate([zeros, q_t[i][d:]], axis=0))
           for i in chunks]
    s_t = [[(_dot(k_ref[tok(i), :], q_h[i][hh]) * intra[hh]).astype(BF16) for hh in heads] for i in chunks]
    w_t = [jnp.concatenate([_dot(vt_ref[hh * d:(hh + 1) * d, tok(i)], s_t[i][hh]) for hh in heads], axis=0)
           for i in chunks]
    upd = [jnp.where(same_head,
                     _dot(vt_ref[:, tok(i)], (k_ref[tok(i), :].astype(F32) * tail).astype(BF16)), 0.0)
           for i in chunks]
    states = [jnp.zeros((LANES, LANES), F32)]
    for i in range(n - 1):
        states.append(states[i] * decay + upd[i])
    gn = gn_ref[...]
    for i in chunks:
        o_t = w_t[i] if i == 0 else _dot(states[i].astype(BF16), q_t[i]) * inner_t + w_t[i]
        normed = []
        for hh in heads:
            o_h = o_t[hh * d:(hh + 1) * d]
            dlt = o_h - jnp.mean(o_h, axis=0, keepdims=True)
            var = jnp.mean(dlt * dlt, axis=0, keepdims=True)
            normed.append(dlt * lax.rsqrt(var + EPS))
        on = jnp.concatenate(normed, axis=0).T
        o_ref[tok(i), :] = (g_ref[tok(i), :].astype(F32) * (on * gn)).astype(BF16)


def _retention(log_gamma, q_rt, k_r, v_rt, g_r, g_gn, batch, seq):
    t = k_r.shape[0]
    row_blk = pl.BlockSpec((seq, LANES), lambda b, hp: (b, hp))
    col_blk = pl.BlockSpec((LANES, seq), lambda b, hp: (hp, b))
    return pl.pallas_call(
        _ret_kernel,
        grid=(batch, RET_WIDTH // LANES),
        in_specs=[
            pl.BlockSpec(memory_space=pltpu.SMEM),
            col_blk, row_blk, col_blk, row_blk,
            pl.BlockSpec((1, LANES), lambda b, hp: (0, hp)),
        ],
        out_specs=row_blk,
        out_shape=jax.ShapeDtypeStruct((t, RET_WIDTH), BF16),
        compiler_params=pltpu.CompilerParams(
            dimension_semantics=("parallel", "parallel"), vmem_limit_bytes=VMEM_LIMIT),
        name="retention",
    )(log_gamma, q_rt, k_r, v_rt, g_r, g_gn)


def _mla_kernel(qt_ref, k_ref, vt_ref, o_ref, m_ref, acc_ref):
    tile = TQ_MLA
    n_tiles = k_ref.shape[0] // tile
    key = lax.broadcasted_iota(jnp.int32, (tile, tile), 0)
    qry = lax.broadcasted_iota(jnp.int32, (tile, tile), 1)
    diag_ok = key <= qry
    heads = (0, 1)
    dv = MLA_V_DIM
    ones = jnp.ones((ONES_ROWS, tile), BF16)

    def scores(d, hh):
        return _dot(k_ref[d * tile:(d + 1) * tile, hh * LANES:(hh + 1) * LANES],
                    qt_ref[hh * LANES:(hh + 1) * LANES, d * tile:])

    def update(d, hh, s, cols):
        v_ext = jnp.concatenate([vt_ref[hh * dv:(hh + 1) * dv, d * tile:(d + 1) * tile], ones], axis=0)
        m_t = jnp.max(s, axis=0, keepdims=True)
        if d == 0:
            m_ref[hh, :, cols] = m_t
            acc_ref[hh, :, cols] = _dot(v_ext, jnp.exp2(s - m_t).astype(BF16))
        else:
            m = m_ref[hh, :, cols]
            m_new = jnp.maximum(m, m_t)
            m_ref[hh, :, cols] = m_new
            pv = _dot(v_ext, jnp.exp2(s - m_new).astype(BF16))
            acc_ref[hh, :, cols] = jnp.exp2(m - m_new) * acc_ref[hh, :, cols] + pv

    s_next = {hh: scores(0, hh) for hh in heads}
    for d in range(n_tiles):
        done = slice(d * tile, (d + 1) * tile)
        for hh in heads:
            s = s_next[hh]
            if d + 1 < n_tiles:
                s_next[hh] = scores(d + 1, hh)
            update(d, hh, jnp.where(diag_ok, s[:, :tile], -jnp.inf), done)
            for c0 in range(d + 1, n_tiles, COL_TILES_MLA):
                c1 = min(c0 + COL_TILES_MLA, n_tiles)
                update(d, hh, s[:, (c0 - d) * tile:(c1 - d) * tile], slice(c0 * tile, c1 * tile))
        o_t = jnp.concatenate([acc_ref[hh, :dv, done] / acc_ref[hh, dv:dv + 1, done] for hh in heads], axis=0)
        o_ref[done, :] = o_t.T.astype(BF16)


def _mla(q_t, k_m, v_t, batch, seq):
    t = k_m.shape[0]
    return pl.pallas_call(
        _mla_kernel,
        grid=(batch, MLA_WIDTH // LANES),
        in_specs=[
            pl.BlockSpec((2 * LANES, seq), lambda b, hp: (hp, b)),
            pl.BlockSpec((seq, 2 * LANES), lambda b, hp: (b, hp)),
            pl.BlockSpec((LANES, seq), lambda b, hp: (hp, b)),
        ],
        out_specs=pl.BlockSpec((seq, LANES), lambda b, hp: (b, hp)),
        out_shape=jax.ShapeDtypeStruct((t, MLA_WIDTH), BF16),
        scratch_shapes=[
            pltpu.VMEM((2, 1, seq), F32),
            pltpu.VMEM((2, MLA_V_DIM + ONES_ROWS, seq), F32),
        ],
        compiler_params=pltpu.CompilerParams(
            dimension_semantics=("parallel", "parallel"), vmem_limit_bytes=VMEM_LIMIT),
        name="mla_attention",
    )(q_t, k_m, v_t)


def _causal_conv(u, cw, halo):
    row = lax.broadcasted_iota(jnp.int32, halo.shape, 0)

    def shifted(n):
        r = pltpu.roll(u, n, 0)
        first = jnp.where(row < n, pltpu.roll(halo, n, 0), r[:SUBLANES])
        return jnp.concatenate([first, r[SUBLANES:]], axis=0)

    return cw[0:1] * shifted(2) + cw[1:2] * shifted(1) + cw[2:3] * u + cw[3:4]


def _post_kernel(x_ref, yr_ref, ym_ref, p_ref, wo_ref, gpm_ref, gpf_ref, wup_ref, cw_ref, wdn_ref,
                 gff_ref, wple_ref, wgate_ref, bgate_ref, gple_ref, out_ref,
                 halo_ref, x1_ref, h2_ref, f_ref):
    tm = x_ref.shape[0]

    @pl.when(pl.program_id(1) == 0)
    def _():
        halo_ref[...] = jnp.zeros_like(halo_ref)

    n_blk = tm // ROWS_POST
    n_units = N_FF_CHUNKS * n_blk
    half = D_MODEL // 2

    def ff_cols(j):
        return slice(j * FF_CHUNK, (j + 1) * FF_CHUNK)

    def rows_of(u):
        r0 = (u % n_blk) * ROWS_POST
        return slice(r0, r0 + ROWS_POST)

    mix = _dot(yr_ref[...], wo_ref[0:RET_WIDTH, :]) + _dot(ym_ref[...], wo_ref[RET_WIDTH:, :])
    x1 = x_ref[...] + _rms(mix, gpm_ref[...])
    x1_ref[...] = x1
    h2_ref[...] = _rms(x1, gpf_ref[...]).astype(BF16)
    e_in = _dot(p_ref[...].astype(BF16), wple_ref[...])

    def up_gate(u):
        return _dot(h2_ref[rows_of(u), :], wup_ref[:, ff_cols(u // n_blk)])

    def up_val(u):
        return _dot(h2_ref[rows_of(u), :], wup_ref[:, ff_cols(u // n_blk + N_FF_CHUNKS)])

    def geglu(u, ug, uu, prev, q):
        c, b = divmod(u, n_blk)
        qr = ROWS_POST // 4
        r0 = q * qr
        if r0 > 0:
            halo_g, halo_u = ug[r0 - SUBLANES:r0], uu[r0 - SUBLANES:r0]
        elif b > 0:
            halo_g, halo_u = prev[0][ROWS_POST - SUBLANES:], prev[1][ROWS_POST - SUBLANES:]
        else:
            halo_g, halo_u = halo_ref[:, ff_cols(c)], halo_ref[:, ff_cols(c + N_FF_CHUNKS)]
        gate = _causal_conv(ug[r0:r0 + qr], cw_ref[:, ff_cols(c)], halo_g)
        up = _causal_conv(uu[r0:r0 + qr], cw_ref[:, ff_cols(c + N_FF_CHUNKS)], halo_u)
        inner = gate * (GELU_C1 + GELU_C2 * (gate * gate))
        return ((gate * up) * (1.0 + jnp.tanh(inner))).astype(BF16)

    def down(u, act, n):
        c = u // n_blk
        cols = slice(n * half, (n + 1) * half)
        dn = _dot(act, wdn_ref[ff_cols(c), cols])
        if c == 0:
            f_ref[rows_of(u), cols] = dn
        else:
            f_ref[rows_of(u), cols] += dn

    cur = (up_gate(0), up_val(0))
    prev = None
    act_prev = None
    for u in range(n_units):
        more = u + 1 < n_units
        ug, uu = cur
        nxt_g = up_gate(u + 1) if more else None
        a0 = geglu(u, ug, uu, prev, 0)
        nxt_u = up_val(u + 1) if more else None
        a1 = geglu(u, ug, uu, prev, 1)
        if act_prev is not None:
            down(u - 1, act_prev, 0)
        a2 = geglu(u, ug, uu, prev, 2)
        if act_prev is not None:
            down(u - 1, act_prev, 1)
        a3 = geglu(u, ug, uu, prev, 3)
        act_prev = jnp.concatenate([a0, a1, a2, a3], axis=0)
        if u % n_blk == n_blk - 1:
            c = u // n_blk
            halo_ref[:, ff_cols(c)] = ug[ROWS_POST - SUBLANES:]
            halo_ref[:, ff_cols(c + N_FF_CHUNKS)] = uu[ROWS_POST - SUBLANES:]
        prev = cur
        cur = (nxt_g, nxt_u)
    down(n_units - 1, act_prev, 0)
    down(n_units - 1, act_prev, 1)

    for b in range(n_blk):
        rows = rows_of(b)
        x2 = x1_ref[rows, :] + _rms(f_ref[rows, :], gff_ref[...])
        ple_gate = jax.nn.sigmoid(_dot(x2.astype(BF16), wgate_ref[...]) + bgate_ref[...])
        out_ref[rows, :] = x2 + _rms(e_in[rows] * ple_gate, gple_ref[...])


def _post(x2d, y_ret, y_mla, p2d, w_o, g_pm, g_pf, w_up_c, cw_c, w_dn_c, g_ff, w_ple, w_gate, b_gate,
          g_ple, batch, seq):
    t = x2d.shape[0]
    tm = TM_POST
    ns = seq // tm
    row = lambda b, s: (b * ns + s, 0)
    consts = (w_o, g_pm, g_pf, w_up_c, cw_c, w_dn_c, g_ff, w_ple, w_gate, b_gate, g_ple)
    return pl.pallas_call(
        _post_kernel,
        grid=(batch, ns),
        in_specs=[
            pl.BlockSpec((tm, D_MODEL), row),
            pl.BlockSpec((tm, RET_WIDTH), row),
            pl.BlockSpec((tm, MLA_WIDTH), row),
            pl.BlockSpec((tm, PLE_DIM), row),
        ] + [_const_spec(a.shape) for a in consts],
        out_specs=pl.BlockSpec((tm, D_MODEL), row),
        out_shape=jax.ShapeDtypeStruct((t, D_MODEL), F32),
        scratch_shapes=[
            pltpu.VMEM((SUBLANES, 2 * D_FF), F32),
            pltpu.VMEM((tm, D_MODEL), F32),
            pltpu.VMEM((tm, D_MODEL), BF16),
            pltpu.VMEM((tm, D_MODEL), F32),
        ],
        compiler_params=pltpu.CompilerParams(
            dimension_semantics=("arbitrary", "arbitrary"), vmem_limit_bytes=VMEM_LIMIT),
        name="post",
    )(x2d, y_ret, y_mla, p2d, *consts)


def _layer(x2d, p2d, pos3, batch, seq, w_in, w_uq, w_ukv, w_o, g_pre_mix, g_post_mix, g_q_a, g_kv_a,
           g_ret_gn, w_up, conv_w, conv_b, w_down, g_pre_ffn, g_post_ffn, w_ple, w_ple_gate,
           b_ple_gate, g_post_ple):
    row = lambda v: v.reshape(1, -1)

    kpe_blk = jnp.pad(w_in[:, IN_USED:], ((0, 0), (MLA_NOPE_DIM, LANES - MLA_QK_DIM)))
    w_in_p = jnp.concatenate([w_in[:, RET_WIDTH:2 * RET_WIDTH], w_in[:, 3 * RET_WIDTH:IN_USED], kpe_blk],
                             axis=1).astype(BF16)
    w_q_t = w_in[:, :RET_WIDTH].T.astype(BF16)
    w_v_t = w_in[:, 2 * RET_WIDTH:3 * RET_WIDTH].T.astype(BF16)
    w_uq_t = jnp.pad(w_uq.reshape(MLA_Q_RANK, MLA_HEADS, MLA_QK_DIM),
                     ((0, 0), (0, 0), (0, LANES - MLA_QK_DIM))).reshape(MLA_Q_RANK, MLA_PAD).T.astype(BF16)
    w_ukv3 = w_ukv.reshape(MLA_KV_RANK, MLA_HEADS, MLA_NOPE_DIM + MLA_V_DIM)
    w_uk_p = jnp.pad(w_ukv3[..., :MLA_NOPE_DIM],
                     ((0, 0), (0, 0), (0, LANES - MLA_NOPE_DIM))).reshape(MLA_KV_RANK, MLA_PAD).astype(BF16)
    w_uv_t = w_ukv3[..., MLA_NOPE_DIM:].reshape(MLA_KV_RANK, MLA_WIDTH).T.astype(BF16)
    w_up_c = w_up.astype(BF16)
    w_dn_c = (0.5 * w_down).astype(BF16)
    cw4 = jnp.concatenate([conv_w, conv_b[None, :]], axis=0)
    cw_c = jnp.pad(cw4, ((0, SUBLANES - 4), (0, 0)))

    invf_r = ROPE_BASE ** (-jnp.arange(0, RET_HEAD_DIM, 2, dtype=F32) / RET_HEAD_DIM)
    invf_m = ROPE_BASE ** (-jnp.arange(0, MLA_ROPE_DIM, 2, dtype=F32) / MLA_ROPE_DIM)
    invf = jnp.concatenate([invf_r, invf_m]).reshape(-1, 1)
    log_gamma = jnp.log1p(-jnp.exp2(-5.0 - jnp.arange(RET_HEADS, dtype=F32)))

    k_r, g_r, k_m, q_rt, v_rt, q_t, v_t = _proj(
        pos3, x2d, invf, row(g_pre_mix), w_in_p, row(g_q_a), w_uq_t, row(g_kv_a), w_uk_p, w_uv_t, w_q_t, w_v_t)
    y_ret = _retention(log_gamma, q_rt, k_r, v_rt, g_r, row(g_ret_gn), batch, seq)
    y_mla = _mla(q_t, k_m, v_t, batch, seq)
    return _post(x2d, y_ret, y_mla, p2d, w_o.astype(BF16), row(g_post_mix), row(g_pre_ffn), w_up_c, cw_c,
                 w_dn_c, row(g_post_ffn), w_ple.astype(BF16), w_ple_gate.astype(BF16), row(b_ple_gate),
                 row(g_post_ple), batch, seq)


def kernel(x, p, positions, w_in, w_uq, w_ukv, w_o, g_pre_mix, g_post_mix, g_q_a, g_kv_a, g_ret_gn,
           w_up, conv_w, conv_b, w_down, g_pre_ffn, g_post_ffn, w_ple, w_ple_gate, b_ple_gate,
           g_post_ple):
    batch, seq, _ = x.shape
    depth = w_in.shape[0]
    assert seq % TM_PROJ == 0 and seq % TM_POST == 0 and seq % TQ_MLA == 0 and seq % RET_CHUNK == 0
    t = batch * seq
    x2d = x.reshape(t, D_MODEL)
    pos3 = positions.reshape(t // TM_PROJ, 1, TM_PROJ)
    for i in range(depth):
        x2d = _layer(x2d, p[i].reshape(t, PLE_DIM), pos3, batch, seq, w_in[i], w_uq[i], w_ukv[i], w_o[i],
                     g_pre_mix[i], g_post_mix[i], g_q_a[i], g_kv_a[i], g_ret_gn[i], w_up[i], conv_w[i],
                     conv_b[i], w_down[i], g_pre_ffn[i], g_post_ffn[i], w_ple[i], w_ple_gate[i],
                     b_ple_gate[i], g_post_ple[i])
    return x2d.reshape(batch, seq, D_MODEL)
```

```python
import jax
import jax.numpy as jnp
from jax import lax
from jax.experimental import pallas as pl
from jax.experimental.pallas import tpu as pltpu

F32 = jnp.float32
BF16 = jnp.bfloat16

D_MODEL = 1024
PLE_DIM = 256
RET_HEADS = 8
RET_HEAD_DIM = 64
RET_WIDTH = RET_HEADS * RET_HEAD_DIM
RET_CHUNK = 128
MLA_HEADS = 8
MLA_NOPE_DIM = 64
MLA_ROPE_DIM = 32
MLA_QK_DIM = MLA_NOPE_DIM + MLA_ROPE_DIM
MLA_V_DIM = 64
MLA_Q_RANK = 256
MLA_KV_RANK = 128
MLA_WIDTH = MLA_HEADS * MLA_V_DIM
D_FF = 2816
ROPE_BASE = 10000.0
EPS = 1e-6
LOG2_E = 1.4426950408889634
GELU_C1 = 0.7978845608028654
GELU_C2 = GELU_C1 * 0.044715

LANES = 128
SUBLANES = 8
IN_USED = 4 * RET_WIDTH + MLA_Q_RANK + MLA_KV_RANK
MLA_PAD = MLA_HEADS * LANES
N_FREQ_RET = RET_HEAD_DIM // 2
N_FREQ_MLA = MLA_ROPE_DIM // 2
FF_CHUNK = 256
FF_BOUNDS = tuple(range(0, D_FF, FF_CHUNK)) + (D_FF,)
N_FF_CHUNKS = len(FF_BOUNDS) - 1

TM_PROJ = 512
TM_POST = 512
ROWS_POST = 256
TQ_MLA = 256
COL_TILES_MLA = 1
PAIRS_MLA = 2
PAIRS_RET = 4
ONES_ROWS = 16
VMEM_LIMIT = 56 * 1024 * 1024


def _dot(a, b):
    return jnp.dot(a, b, preferred_element_type=F32)


def _rms(x, g):
    return x * lax.rsqrt(jnp.mean(x * x, axis=-1, keepdims=True) + EPS) * g


def _const_spec(shape):
    zeros = (0,) * len(shape)
    return pl.BlockSpec(shape, lambda *_: zeros, pipeline_mode=pl.Buffered(1))


def _proj_kernel(pos_ref, x_ref, invf_ref, gpre_ref, win_ref, gqa_ref, wuqt_ref, gkva_ref, wuk_ref, wuvt_ref,
                 wqt_ref, wvt_ref, kr_ref, gr_ref, km_ref, qrt_ref, vrt_ref, qt_ref, vt_ref):
    tm = x_ref.shape[0]
    h_f32 = _rms(x_ref[...], gpre_ref[...])
    h = h_f32.astype(BF16)

    zc = _dot(h, win_ref[:, 2 * RET_WIDTH:])
    zk = _dot(h, win_ref[:, 0:RET_WIDTH])
    zg = _dot(h, win_ref[:, RET_WIDTH:2 * RET_WIDTH])

    ang = invf_ref[...] * pos_ref[0].astype(F32)
    cos_t = jnp.cos(ang)
    sin_t = jnp.sin(ang)
    cos_rt, sin_rt = cos_t[:N_FREQ_RET], sin_t[:N_FREQ_RET]
    cos_mt, sin_mt = cos_t[N_FREQ_RET:], sin_t[N_FREQ_RET:]

    h_t = h_f32.T.astype(BF16)
    q_rt = _dot(wqt_ref[...], h_t)
    v_rt = _dot(wvt_ref[...], h_t)

    cq = _rms(zc[:, :MLA_Q_RANK], gqa_ref[...])
    ckv = _rms(zc[:, MLA_Q_RANK:MLA_Q_RANK + MLA_KV_RANK], gkva_ref[...])
    q_t = _dot(wuqt_ref[...], cq.T.astype(BF16))
    k_up = _dot(ckv.astype(BF16), wuk_ref[...])
    v_t = _dot(wuvt_ref[...], ckv.T.astype(BF16))

    gr_ref[...] = jax.nn.silu(zg).astype(BF16)

    reps_r = LANES // N_FREQ_RET
    reps_m = LANES // N_FREQ_MLA
    cos_r = jnp.concatenate([cos_rt] * reps_r, axis=0).T
    sin_r = jnp.concatenate([sin_rt] * reps_r, axis=0).T
    lane = lax.broadcasted_iota(jnp.int32, (1, LANES), 1)
    first_half = (lane & (RET_HEAD_DIM - 1)) < N_FREQ_RET
    sin_r_lo = jnp.where(first_half, -sin_r, 0.0)
    sin_r_hi = jnp.where(first_half, 0.0, sin_r)
    k_scale = RET_HEAD_DIM ** -0.5
    for c in range(RET_WIDTH // LANES):
        sl = slice(c * LANES, (c + 1) * LANES)
        blk = zk[:, sl]
        roped = (blk * cos_r + pltpu.roll(blk, LANES - N_FREQ_RET, 1) * sin_r_lo
                 + pltpu.roll(blk, N_FREQ_RET, 1) * sin_r_hi)
        kr_ref[:, sl] = (roped * k_scale).astype(BF16)

    for hh in range(RET_HEADS):
        b0 = hh * RET_HEAD_DIM
        x1 = q_rt[b0:b0 + N_FREQ_RET]
        x2 = q_rt[b0 + N_FREQ_RET:b0 + RET_HEAD_DIM]
        qrt_ref[b0:b0 + RET_HEAD_DIM, :] = jnp.concatenate(
            [x1 * cos_rt - x2 * sin_rt, x2 * cos_rt + x1 * sin_rt], axis=0).astype(BF16)
    vrt_ref[...] = v_rt.astype(BF16)

    r0 = MLA_NOPE_DIM
    r1 = MLA_NOPE_DIM + N_FREQ_MLA
    r2 = MLA_NOPE_DIM + MLA_ROPE_DIM
    cos_m = jnp.concatenate([cos_mt] * reps_m, axis=0).T
    sin_m = jnp.concatenate([sin_mt] * reps_m, axis=0).T
    in_rot = (lane >= r0) & (lane < r2)
    cos_m_f = jnp.where(in_rot, cos_m, 0.0)
    sin_m_lo = jnp.where((lane >= r0) & (lane < r1), -sin_m, 0.0)
    sin_m_hi = jnp.where((lane >= r1) & (lane < r2), sin_m, 0.0)
    kpe = zc[:, MLA_Q_RANK + MLA_KV_RANK:]
    k_rot = (kpe * cos_m_f + pltpu.roll(kpe, LANES - N_FREQ_MLA, 1) * sin_m_lo
             + pltpu.roll(kpe, N_FREQ_MLA, 1) * sin_m_hi)
    for hh in range(MLA_HEADS):
        sl = slice(hh * LANES, (hh + 1) * LANES)
        km_ref[:, sl] = (k_up[:, sl] + k_rot).astype(BF16)
    vt_ref[...] = v_t.astype(BF16)

    q_scale = MLA_QK_DIM ** -0.5 * LOG2_E
    pad = jnp.zeros((LANES - r2, tm), F32)
    for hh in range(MLA_HEADS):
        b0 = hh * LANES
        x1 = q_t[b0 + r0:b0 + r1]
        x2 = q_t[b0 + r1:b0 + r2]
        blk = jnp.concatenate([q_t[b0:b0 + r0], x1 * cos_mt - x2 * sin_mt, x2 * cos_mt + x1 * sin_mt, pad],
                              axis=0)
        qt_ref[b0:b0 + LANES, :] = (blk * q_scale).astype(BF16)


def _proj(pos3, x2d, invf, g_pre, w_in_p, g_qa, w_uq_t, g_kva, w_uk_p, w_uv_t, w_q_t, w_v_t):
    t = x2d.shape[0]
    tm = TM_PROJ
    row = lambda i: (i, 0)
    col = lambda i: (0, i)
    row_widths = (RET_WIDTH, RET_WIDTH, MLA_PAD)
    col_heights = (RET_WIDTH, RET_WIDTH, MLA_PAD, MLA_WIDTH)
    return pl.pallas_call(
        _proj_kernel,
        grid=(t // tm,),
        in_specs=[
            pl.BlockSpec((1, 1, tm), lambda i: (i, 0, 0)),
            pl.BlockSpec((tm, D_MODEL), row),
            _const_spec(invf.shape),
            _const_spec(g_pre.shape),
            _const_spec(w_in_p.shape),
            _const_spec(g_qa.shape),
            _const_spec(w_uq_t.shape),
            _const_spec(g_kva.shape),
            _const_spec(w_uk_p.shape),
            _const_spec(w_uv_t.shape),
            _const_spec(w_q_t.shape),
            _const_spec(w_v_t.shape),
        ],
        out_specs=([pl.BlockSpec((tm, w), row) for w in row_widths]
                   + [pl.BlockSpec((hgt, tm), col) for hgt in col_heights]),
        out_shape=([jax.ShapeDtypeStruct((t, w), BF16) for w in row_widths]
                   + [jax.ShapeDtypeStruct((hgt, t), BF16) for hgt in col_heights]),
        compiler_params=pltpu.CompilerParams(
            dimension_semantics=("parallel",), vmem_limit_bytes=VMEM_LIMIT),
        name="proj",
    )(pos3, x2d, invf, g_pre, w_in_p, g_qa, w_uq_t, g_kva, w_uk_p, w_uv_t, w_q_t, w_v_t)


def _ret_kernel(lg_ref, qt_ref, k_ref, vt_ref, g_ref, gn_ref, o_ref):
    for pair in range(PAIRS_RET):
        _ret_pair(pl.program_id(1) * PAIRS_RET + pair, slice(pair * LANES, (pair + 1) * LANES),
                  lg_ref, qt_ref, k_ref, vt_ref, g_ref, gn_ref, o_ref)


def _ret_pair(hp, span, lg_ref, qt_ref, k_ref, vt_ref, g_ref, gn_ref, o_ref):
    qt_ref, vt_ref = qt_ref.at[span, :], vt_ref.at[span, :]
    k_ref, g_ref, gn_ref, o_ref = k_ref.at[:, span], g_ref.at[:, span], gn_ref.at[:, span], o_ref.at[:, span]
    c = RET_CHUNK
    d = RET_HEAD_DIM
    n = k_ref.shape[0] // c
    lg0 = lg_ref[2 * hp]
    lg1 = lg_ref[2 * hp + 1]
    lane = lax.broadcasted_iota(jnp.int32, (1, LANES), 1)
    sub = lax.broadcasted_iota(jnp.int32, (LANES, 1), 0)
    lg_lane = jnp.where(lane < d, lg0, lg1)
    lg_sub = jnp.where(sub < d, lg0, lg1)
    key = lax.broadcasted_iota(jnp.int32, (c, c), 0)
    qry = lax.broadcasted_iota(jnp.int32, (c, c), 1)
    rel = (qry - key).astype(F32)
    causal = rel >= 0
    relp = jnp.maximum(rel, 0.0)
    intra = (jnp.where(causal, jnp.exp(lg0 * relp), 0.0), jnp.where(causal, jnp.exp(lg1 * relp), 0.0))
    tok_lane = lax.broadcasted_iota(jnp.int32, (1, c), 1).astype(F32)
    tok_sub = lax.broadcasted_iota(jnp.int32, (c, 1), 0).astype(F32)
    inner_t = jnp.exp(lg_sub * (tok_lane + 1.0))
    tail = jnp.exp(lg_lane * (c - 1.0 - tok_sub))
    decay = jnp.exp(lg_sub * float(c))
    same_head = (key < d) == (qry < d)
    zeros = jnp.zeros((d, c), BF16)
    heads = (0, 1)
    chunks = range(n)

    def tok(i):
        return slice(i * c, (i + 1) * c)

    q_t = [qt_ref[:, tok(i)] for i in chunks]
    q_h = [(jnp.concatenate([q_t[i][:d], zeros], axis=0), jnp.concatenate([zeros, q_t[i][d:]], axis=0))
           for i in chunks]
    s_t = [[(_dot(k_ref[tok(i), :], q_h[i][hh]) * intra[hh]).astype(BF16) for hh in heads] for i in chunks]
    w_t = [jnp.concatenate([_dot(vt_ref[hh * d:(hh + 1) * d, tok(i)], s_t[i][hh]) for hh in heads], axis=0)
           for i in chunks]
    upd = [jnp.where(same_head,
                     _dot(vt_ref[:, tok(i)], (k_ref[tok(i), :].astype(F32) * tail).astype(BF16)), 0.0)
           for i in chunks]
    states = [jnp.zeros((LANES, LANES), F32)]
    for i in range(n - 1):
        states.append(states[i] * decay + upd[i])
    gn = gn_ref[...]
    for i in chunks:
        o_t = w_t[i] if i == 0 else _dot(states[i].astype(BF16), q_t[i]) * inner_t + w_t[i]
        normed = []
        for hh in heads:
            o_h = o_t[hh * d:(hh + 1) * d]
            dlt = o_h - jnp.mean(o_h, axis=0, keepdims=True)
            var = jnp.mean(dlt * dlt, axis=0, keepdims=True)
            normed.append(dlt * lax.rsqrt(var + EPS))
        on = jnp.concatenate(normed, axis=0).T
        o_ref[tok(i), :] = (g_ref[tok(i), :].astype(F32) * (on * gn)).astype(BF16)


def _retention(log_gamma, q_rt, k_r, v_rt, g_r, g_gn, batch, seq):
    t = k_r.shape[0]
    width = PAIRS_RET * LANES
    row_blk = pl.BlockSpec((seq, width), lambda b, hp: (b, hp))
    col_blk = pl.BlockSpec((width, seq), lambda b, hp: (hp, b))
    return pl.pallas_call(
        _ret_kernel,
        grid=(batch, RET_WIDTH // width),
        in_specs=[
            pl.BlockSpec(memory_space=pltpu.SMEM),
            col_blk, row_blk, col_blk, row_blk,
            pl.BlockSpec((1, width), lambda b, hp: (0, hp)),
        ],
        out_specs=row_blk,
        out_shape=jax.ShapeDtypeStruct((t, RET_WIDTH), BF16),
        compiler_params=pltpu.CompilerParams(
            dimension_semantics=("parallel", "parallel"), vmem_limit_bytes=VMEM_LIMIT),
        name="retention",
    )(log_gamma, q_rt, k_r, v_rt, g_r, g_gn)


def _mla_kernel(qt_ref, k_ref, vt_ref, o_ref, m_ref, acc_ref):
    for pair in range(PAIRS_MLA):
        qk = slice(pair * 2 * LANES, (pair + 1) * 2 * LANES)
        vo = slice(pair * LANES, (pair + 1) * LANES)
        st = slice(pair * 2, (pair + 1) * 2)
        _mla_pair(qt_ref.at[qk, :], k_ref.at[:, qk], vt_ref.at[vo, :], o_ref.at[:, vo], m_ref.at[st], acc_ref.at[st])


def _mla_pair(qt_ref, k_ref, vt_ref, o_ref, m_ref, acc_ref):
    tile = TQ_MLA
    n_tiles = k_ref.shape[0] // tile
    key = lax.broadcasted_iota(jnp.int32, (tile, tile), 0)
    qry = lax.broadcasted_iota(jnp.int32, (tile, tile), 1)
    diag_ok = key <= qry
    heads = (0, 1)
    dv = MLA_V_DIM
    ones = jnp.ones((ONES_ROWS, tile), BF16)

    def scores(d, hh):
        return _dot(k_ref[d * tile:(d + 1) * tile, hh * LANES:(hh + 1) * LANES],
                    qt_ref[hh * LANES:(hh + 1) * LANES, d * tile:])

    def update(d, hh, s, cols):
        v_ext = jnp.concatenate([vt_ref[hh * dv:(hh + 1) * dv, d * tile:(d + 1) * tile], ones], axis=0)
        m_t = jnp.max(s, axis=0, keepdims=True)
        if d == 0:
            m_ref[hh, :, cols] = m_t
            acc_ref[hh, :, cols] = _dot(v_ext, jnp.exp2(s - m_t).astype(BF16))
        else:
            m = m_ref[hh, :, cols]
            m_new = jnp.maximum(m, m_t)
            m_ref[hh, :, cols] = m_new
            pv = _dot(v_ext, jnp.exp2(s - m_new).astype(BF16))
            acc_ref[hh, :, cols] = jnp.exp2(m - m_new) * acc_ref[hh, :, cols] + pv

    s_next = {hh: scores(0, hh) for hh in heads}
    for d in range(n_tiles):
        done = slice(d * tile, (d + 1) * tile)
        for hh in heads:
            s = s_next[hh]
            if d + 1 < n_tiles:
                s_next[hh] = scores(d + 1, hh)
            update(d, hh, jnp.where(diag_ok, s[:, :tile], -jnp.inf), done)
            for c0 in range(d + 1, n_tiles, COL_TILES_MLA):
                c1 = min(c0 + COL_TILES_MLA, n_tiles)
                update(d, hh, s[:, (c0 - d) * tile:(c1 - d) * tile], slice(c0 * tile, c1 * tile))
        o_t = jnp.concatenate([acc_ref[hh, :dv, done] / acc_ref[hh, dv:dv + 1, done] for hh in heads], axis=0)
        o_ref[done, :] = o_t.T.astype(BF16)


def _mla(q_t, k_m, v_t, batch, seq):
    t = k_m.shape[0]
    return pl.pallas_call(
        _mla_kernel,
        grid=(batch, MLA_WIDTH // (PAIRS_MLA * LANES)),
        in_specs=[
            pl.BlockSpec((PAIRS_MLA * 2 * LANES, seq), lambda b, hp: (hp, b)),
            pl.BlockSpec((seq, PAIRS_MLA * 2 * LANES), lambda b, hp: (b, hp)),
            pl.BlockSpec((PAIRS_MLA * LANES, seq), lambda b, hp: (hp, b)),
        ],
        out_specs=pl.BlockSpec((seq, PAIRS_MLA * LANES), lambda b, hp: (b, hp)),
        out_shape=jax.ShapeDtypeStruct((t, MLA_WIDTH), BF16),
        scratch_shapes=[
            pltpu.VMEM((PAIRS_MLA * 2, 1, seq), F32),
            pltpu.VMEM((PAIRS_MLA * 2, MLA_V_DIM + ONES_ROWS, seq), F32),
        ],
        compiler_params=pltpu.CompilerParams(
            dimension_semantics=("parallel", "parallel"), vmem_limit_bytes=VMEM_LIMIT),
        name="mla_attention",
    )(q_t, k_m, v_t)


def _causal_conv(u, cw, halo):
    row = lax.broadcasted_iota(jnp.int32, halo.shape, 0)

    def shifted(n):
        r = pltpu.roll(u, n, 0)
        first = jnp.where(row < n, pltpu.roll(halo, n, 0), r[:SUBLANES])
        return jnp.concatenate([first, r[SUBLANES:]], axis=0)

    return cw[0:1] * shifted(2) + cw[1:2] * shifted(1) + cw[2:3] * u + cw[3:4]


def _post_kernel(x_ref, yr_ref, ym_ref, p_ref, wo_ref, gpm_ref, gpf_ref, wup_ref, cw_ref, wdn_ref,
                 gff_ref, wple_ref, wgate_ref, bgate_ref, gple_ref, out_ref,
                 halo_ref, x1_ref, h2_ref, f_ref):
    tm = x_ref.shape[0]

    @pl.when(pl.program_id(1) == 0)
    def _():
        halo_ref[...] = jnp.zeros_like(halo_ref)

    n_blk = tm // ROWS_POST
    n_units = N_FF_CHUNKS * n_blk
    half = D_MODEL // 2

    def ff_cols(j, up=False):
        off = D_FF if up else 0
        return slice(off + FF_BOUNDS[j], off + FF_BOUNDS[j + 1])

    def rows_of(u):
        r0 = (u % n_blk) * ROWS_POST
        return slice(r0, r0 + ROWS_POST)

    mix = _dot(yr_ref[...], wo_ref[0:RET_WIDTH, :]) + _dot(ym_ref[...], wo_ref[RET_WIDTH:, :])
    x1 = x_ref[...] + _rms(mix, gpm_ref[...])
    x1_ref[...] = x1
    h2_ref[...] = _rms(x1, gpf_ref[...]).astype(BF16)
    e_in = _dot(p_ref[...].astype(BF16), wple_ref[...])

    def up_gate(u):
        return _dot(h2_ref[rows_of(u), :], wup_ref[:, ff_cols(u // n_blk)])

    def up_val(u):
        return _dot(h2_ref[rows_of(u), :], wup_ref[:, ff_cols(u // n_blk, up=True)])

    def geglu(u, ug, uu, prev, q):
        c, b = divmod(u, n_blk)
        qr = ROWS_POST // 4
        r0 = q * qr
        if r0 > 0:
            halo_g, halo_u = ug[r0 - SUBLANES:r0], uu[r0 - SUBLANES:r0]
        elif b > 0:
            halo_g, halo_u = prev[0][ROWS_POST - SUBLANES:], prev[1][ROWS_POST - SUBLANES:]
        else:
            halo_g, halo_u = halo_ref[:, ff_cols(c)], halo_ref[:, ff_cols(c, up=True)]
        gate = _causal_conv(ug[r0:r0 + qr], cw_ref[:, ff_cols(c)], halo_g)
        up = _causal_conv(uu[r0:r0 + qr], cw_ref[:, ff_cols(c, up=True)], halo_u)
        inner = gate * (GELU_C1 + GELU_C2 * (gate * gate))
        return ((gate * up) * (1.0 + jnp.tanh(inner))).astype(BF16)

    def down(u, act, n):
        c = u // n_blk
        cols = slice(n * half, (n + 1) * half)
        dn = _dot(act, wdn_ref[ff_cols(c), cols])
        if c == 0:
            f_ref[rows_of(u), cols] = dn
        else:
            f_ref[rows_of(u), cols] += dn

    cur = (up_gate(0), up_val(0))
    prev = None
    act_prev = None
    for u in range(n_units):
        more = u + 1 < n_units
        ug, uu = cur
        nxt_g = up_gate(u + 1) if more else None
        a0 = geglu(u, ug, uu, prev, 0)
        nxt_u = up_val(u + 1) if more else None
        a1 = geglu(u, ug, uu, prev, 1)
        if act_prev is not None:
            down(u - 1, act_prev, 0)
        a2 = geglu(u, ug, uu, prev, 2)
        if act_prev is not None:
            down(u - 1, act_prev, 1)
        a3 = geglu(u, ug, uu, prev, 3)
        act_prev = jnp.concatenate([a0, a1, a2, a3], axis=0)
        if u % n_blk == n_blk - 1:
            c = u // n_blk
            halo_ref[:, ff_cols(c)] = ug[ROWS_POST - SUBLANES:]
            halo_ref[:, ff_cols(c, up=True)] = uu[ROWS_POST - SUBLANES:]
        prev = cur
        cur = (nxt_g, nxt_u)
    down(n_units - 1, act_prev, 0)
    down(n_units - 1, act_prev, 1)

    for b in range(n_blk):
        rows = rows_of(b)
        x2 = x1_ref[rows, :] + _rms(f_ref[rows, :], gff_ref[...])
        ple_gate = jax.nn.sigmoid(_dot(x2.astype(BF16), wgate_ref[...]) + bgate_ref[...])
        out_ref[rows, :] = x2 + _rms(e_in[rows] * ple_gate, gple_ref[...])


def _post(x2d, y_ret, y_mla, p2d, w_o, g_pm, g_pf, w_up_c, cw_c, w_dn_c, g_ff, w_ple, w_gate, b_gate,
          g_ple, batch, seq):
    t = x2d.shape[0]
    tm = TM_POST
    ns = seq // tm
    row = lambda b, s: (b * ns + s, 0)
    consts = (w_o, g_pm, g_pf, w_up_c, cw_c, w_dn_c, g_ff, w_ple, w_gate, b_gate, g_ple)
    return pl.pallas_call(
        _post_kernel,
        grid=(batch, ns),
        in_specs=[
            pl.BlockSpec((tm, D_MODEL), row),
            pl.BlockSpec((tm, RET_WIDTH), row),
            pl.BlockSpec((tm, MLA_WIDTH), row),
            pl.BlockSpec((tm, PLE_DIM), row),
        ] + [_const_spec(a.shape) for a in consts],
        out_specs=pl.BlockSpec((tm, D_MODEL), row),
        out_shape=jax.ShapeDtypeStruct((t, D_MODEL), F32),
        scratch_shapes=[
            pltpu.VMEM((SUBLANES, 2 * D_FF), F32),
            pltpu.VMEM((tm, D_MODEL), F32),
            pltpu.VMEM((tm, D_MODEL), BF16),
            pltpu.VMEM((tm, D_MODEL), F32),
        ],
        compiler_params=pltpu.CompilerParams(
            dimension_semantics=("arbitrary", "arbitrary"), vmem_limit_bytes=VMEM_LIMIT),
        name="post",
    )(x2d, y_ret, y_mla, p2d, *consts)


def _layer(x2d, p2d, pos3, batch, seq, w_in, w_uq, w_ukv, w_o, g_pre_mix, g_post_mix, g_q_a, g_kv_a,
           g_ret_gn, w_up, conv_w, conv_b, w_down, g_pre_ffn, g_post_ffn, w_ple, w_ple_gate,
           b_ple_gate, g_post_ple):
    row = lambda v: v.reshape(1, -1)

    kpe_blk = jnp.pad(w_in[:, IN_USED:], ((0, 0), (MLA_NOPE_DIM, LANES - MLA_QK_DIM)))
    w_in_p = jnp.concatenate([w_in[:, RET_WIDTH:2 * RET_WIDTH], w_in[:, 3 * RET_WIDTH:IN_USED], kpe_blk],
                             axis=1).astype(BF16)
    w_q_t = w_in[:, :RET_WIDTH].T.astype(BF16)
    w_v_t = w_in[:, 2 * RET_WIDTH:3 * RET_WIDTH].T.astype(BF16)
    w_uq_t = jnp.pad(w_uq.reshape(MLA_Q_RANK, MLA_HEADS, MLA_QK_DIM),
                     ((0, 0), (0, 0), (0, LANES - MLA_QK_DIM))).reshape(MLA_Q_RANK, MLA_PAD).T.astype(BF16)
    w_ukv3 = w_ukv.reshape(MLA_KV_RANK, MLA_HEADS, MLA_NOPE_DIM + MLA_V_DIM)
    w_uk_p = jnp.pad(w_ukv3[..., :MLA_NOPE_DIM],
                     ((0, 0), (0, 0), (0, LANES - MLA_NOPE_DIM))).reshape(MLA_KV_RANK, MLA_PAD).astype(BF16)
    w_uv_t = w_ukv3[..., MLA_NOPE_DIM:].reshape(MLA_KV_RANK, MLA_WIDTH).T.astype(BF16)
    w_up_c = w_up.astype(BF16)
    w_dn_c = (0.5 * w_down).astype(BF16)
    cw4 = jnp.concatenate([conv_w, conv_b[None, :]], axis=0)
    cw_c = jnp.pad(cw4, ((0, SUBLANES - 4), (0, 0)))

    invf_r = ROPE_BASE ** (-jnp.arange(0, RET_HEAD_DIM, 2, dtype=F32) / RET_HEAD_DIM)
    invf_m = ROPE_BASE ** (-jnp.arange(0, MLA_ROPE_DIM, 2, dtype=F32) / MLA_ROPE_DIM)
    invf = jnp.concatenate([invf_r, invf_m]).reshape(-1, 1)
    log_gamma = jnp.log1p(-jnp.exp2(-5.0 - jnp.arange(RET_HEADS, dtype=F32)))

    k_r, g_r, k_m, q_rt, v_rt, q_t, v_t = _proj(
        pos3, x2d, invf, row(g_pre_mix), w_in_p, row(g_q_a), w_uq_t, row(g_kv_a), w_uk_p, w_uv_t, w_q_t, w_v_t)
    y_ret = _retention(log_gamma, q_rt, k_r, v_rt, g_r, row(g_ret_gn), batch, seq)
    y_mla = _mla(q_t, k_m, v_t, batch, seq)
    return _post(x2d, y_ret, y_mla, p2d, w_o.astype(BF16), row(g_post_mix), row(g_pre_ffn), w_up_c, cw_c,
                 w_dn_c, row(g_post_ffn), w_ple.astype(BF16), w_ple_gate.astype(BF16), row(b_ple_gate),
                 row(g_post_ple), batch, seq)


def kernel(x, p, positions, w_in, w_uq, w_ukv, w_o, g_pre_mix, g_post_mix, g_q_a, g_kv_a, g_ret_gn,
           w_up, conv_w, conv_b, w_down, g_pre_ffn, g_post_ffn, w_ple, w_ple_gate, b_ple_gate,
           g_post_ple):
    batch, seq, _ = x.shape
    depth = w_in.shape[0]
    assert seq % TM_PROJ == 0 and seq % TM_POST == 0 and seq % TQ_MLA == 0 and seq % RET_CHUNK == 0
    t = batch * seq
    x2d = x.reshape(t, D_MODEL)
    pos3 = positions.reshape(t // TM_PROJ, 1, TM_PROJ)
    for i in range(depth):
        x2d = _layer(x2d, p[i].reshape(t, PLE_DIM), pos3, batch, seq, w_in[i], w_uq[i], w_ukv[i], w_o[i],
                     g_pre_mix[i], g_post_mix[i], g_q_a[i], g_kv_a[i], g_ret_gn[i], w_up[i], conv_w[i],
                     conv_b[i], w_down[i], g_pre_ffn[i], g_post_ffn[i], w_ple[i], w_ple_gate[i],
                     b_ple_gate[i], g_post_ple[i])
    return x2d.reshape(batch, seq, D_MODEL)
```

```python
import jax
import jax.numpy as jnp
from jax import lax
from jax.experimental import pallas as pl
from jax.experimental.pallas import tpu as pltpu

F32 = jnp.float32
BF16 = jnp.bfloat16

D_MODEL = 1024
PLE_DIM = 256
RET_HEADS = 8
RET_HEAD_DIM = 64
RET_WIDTH = RET_HEADS * RET_HEAD_DIM
RET_CHUNK = 128
MLA_HEADS = 8
MLA_NOPE_DIM = 64
MLA_ROPE_DIM = 32
MLA_QK_DIM = MLA_NOPE_DIM + MLA_ROPE_DIM
MLA_V_DIM = 64
MLA_Q_RANK = 256
MLA_KV_RANK = 128
MLA_WIDTH = MLA_HEADS * MLA_V_DIM
D_FF = 2816
ROPE_BASE = 10000.0
EPS = 1e-6
LOG2_E = 1.4426950408889634
GELU_C1 = 0.7978845608028654
GELU_C2 = GELU_C1 * 0.044715

LANES = 128
SUBLANES = 8
IN_USED = 4 * RET_WIDTH + MLA_Q_RANK + MLA_KV_RANK
MLA_PAD = MLA_HEADS * LANES
N_FREQ_RET = RET_HEAD_DIM // 2
N_FREQ_MLA = MLA_ROPE_DIM // 2
FF_CHUNK = 256
FF_BOUNDS = tuple(range(0, D_FF, FF_CHUNK)) + (D_FF,)
N_FF_CHUNKS = len(FF_BOUNDS) - 1

TM_PROJ = 512
TM_POST = 512
ROWS_POST = 128
TQ_MLA = 256
COL_TILES_MLA = 1
PAIRS_MLA = 2
PAIRS_RET = 4
ONES_ROWS = 16
VMEM_LIMIT = 56 * 1024 * 1024


def _dot(a, b):
    return jnp.dot(a, b, preferred_element_type=F32)


def _rms(x, g):
    return x * lax.rsqrt(jnp.mean(x * x, axis=-1, keepdims=True) + EPS) * g


def _const_spec(shape):
    zeros = (0,) * len(shape)
    return pl.BlockSpec(shape, lambda *_: zeros, pipeline_mode=pl.Buffered(1))


def _proj_kernel(pos_ref, x_ref, invf_ref, gpre_ref, win_ref, gqa_ref, wuqt_ref, gkva_ref, wuk_ref, wuvt_ref,
                 wqt_ref, wvt_ref, kr_ref, gr_ref, km_ref, qrt_ref, vrt_ref, qt_ref, vt_ref):
    tm = x_ref.shape[0]
    h_f32 = _rms(x_ref[...], gpre_ref[...])
    h = h_f32.astype(BF16)

    zc = _dot(h, win_ref[:, 2 * RET_WIDTH:])
    zk = _dot(h, win_ref[:, 0:RET_WIDTH])
    zg = _dot(h, win_ref[:, RET_WIDTH:2 * RET_WIDTH])

    ang = invf_ref[...] * pos_ref[0].astype(F32)
    cos_t = jnp.cos(ang)
    sin_t = jnp.sin(ang)
    cos_rt, sin_rt = cos_t[:N_FREQ_RET], sin_t[:N_FREQ_RET]
    cos_mt, sin_mt = cos_t[N_FREQ_RET:], sin_t[N_FREQ_RET:]

    h_t = h_f32.T.astype(BF16)
    q_rt = _dot(wqt_ref[...], h_t)
    v_rt = _dot(wvt_ref[...], h_t)

    cq = _rms(zc[:, :MLA_Q_RANK], gqa_ref[...])
    ckv = _rms(zc[:, MLA_Q_RANK:MLA_Q_RANK + MLA_KV_RANK], gkva_ref[...])
    q_t = _dot(wuqt_ref[...], cq.T.astype(BF16))
    k_up = _dot(ckv.astype(BF16), wuk_ref[...])
    v_t = _dot(wuvt_ref[...], ckv.T.astype(BF16))

    gr_ref[...] = jax.nn.silu(zg).astype(BF16)

    reps_r = LANES // N_FREQ_RET
    reps_m = LANES // N_FREQ_MLA
    cos_r = jnp.concatenate([cos_rt] * reps_r, axis=0).T
    sin_r = jnp.concatenate([sin_rt] * reps_r, axis=0).T
    lane = lax.broadcasted_iota(jnp.int32, (1, LANES), 1)
    first_half = (lane & (RET_HEAD_DIM - 1)) < N_FREQ_RET
    sin_r_lo = jnp.where(first_half, -sin_r, 0.0)
    sin_r_hi = jnp.where(first_half, 0.0, sin_r)
    k_scale = RET_HEAD_DIM ** -0.5
    for c in range(RET_WIDTH // LANES):
        sl = slice(c * LANES, (c + 1) * LANES)
        blk = zk[:, sl]
        roped = (blk * cos_r + pltpu.roll(blk, LANES - N_FREQ_RET, 1) * sin_r_lo
                 + pltpu.roll(blk, N_FREQ_RET, 1) * sin_r_hi)
        kr_ref[:, sl] = (roped * k_scale).astype(BF16)

    for hh in range(RET_HEADS):
        b0 = hh * RET_HEAD_DIM
        x1 = q_rt[b0:b0 + N_FREQ_RET]
        x2 = q_rt[b0 + N_FREQ_RET:b0 + RET_HEAD_DIM]
        qrt_ref[b0:b0 + RET_HEAD_DIM, :] = jnp.concatenate(
            [x1 * cos_rt - x2 * sin_rt, x2 * cos_rt + x1 * sin_rt], axis=0).astype(BF16)
    vrt_ref[...] = v_rt.astype(BF16)

    r0 = MLA_NOPE_DIM
    r1 = MLA_NOPE_DIM + N_FREQ_MLA
    r2 = MLA_NOPE_DIM + MLA_ROPE_DIM
    cos_m = jnp.concatenate([cos_mt] * reps_m, axis=0).T
    sin_m = jnp.concatenate([sin_mt] * reps_m, axis=0).T
    in_rot = (lane >= r0) & (lane < r2)
    cos_m_f = jnp.where(in_rot, cos_m, 0.0)
    sin_m_lo = jnp.where((lane >= r0) & (lane < r1), -sin_m, 0.0)
    sin_m_hi = jnp.where((lane >= r1) & (lane < r2), sin_m, 0.0)
    kpe = zc[:, MLA_Q_RANK + MLA_KV_RANK:]
    k_rot = (kpe * cos_m_f + pltpu.roll(kpe, LANES - N_FREQ_MLA, 1) * sin_m_lo
             + pltpu.roll(kpe, N_FREQ_MLA, 1) * sin_m_hi)
    for hh in range(MLA_HEADS):
        sl = slice(hh * LANES, (hh + 1) * LANES)
        km_ref[:, sl] = (k_up[:, sl] + k_rot).astype(BF16)
    vt_ref[...] = v_t.astype(BF16)

    q_scale = MLA_QK_DIM ** -0.5 * LOG2_E
    pad = jnp.zeros((LANES - r2, tm), F32)
    for hh in range(MLA_HEADS):
        b0 = hh * LANES
        x1 = q_t[b0 + r0:b0 + r1]
        x2 = q_t[b0 + r1:b0 + r2]
        blk = jnp.concatenate([q_t[b0:b0 + r0], x1 * cos_mt - x2 * sin_mt, x2 * cos_mt + x1 * sin_mt, pad],
                              axis=0)
        qt_ref[b0:b0 + LANES, :] = (blk * q_scale).astype(BF16)


def _proj(pos3, x2d, invf, g_pre, w_in_p, g_qa, w_uq_t, g_kva, w_uk_p, w_uv_t, w_q_t, w_v_t):
    t = x2d.shape[0]
    tm = TM_PROJ
    row = lambda i: (i, 0)
    col = lambda i: (0, i)
    row_widths = (RET_WIDTH, RET_WIDTH, MLA_PAD)
    col_heights = (RET_WIDTH, RET_WIDTH, MLA_PAD, MLA_WIDTH)
    return pl.pallas_call(
        _proj_kernel,
        grid=(t // tm,),
        in_specs=[
            pl.BlockSpec((1, 1, tm), lambda i: (i, 0, 0)),
            pl.BlockSpec((tm, D_MODEL), row),
            _const_spec(invf.shape),
            _const_spec(g_pre.shape),
            _const_spec(w_in_p.shape),
            _const_spec(g_qa.shape),
            _const_spec(w_uq_t.shape),
            _const_spec(g_kva.shape),
            _const_spec(w_uk_p.shape),
            _const_spec(w_uv_t.shape),
            _const_spec(w_q_t.shape),
            _const_spec(w_v_t.shape),
        ],
        out_specs=([pl.BlockSpec((tm, w), row) for w in row_widths]
                   + [pl.BlockSpec((hgt, tm), col) for hgt in col_heights]),
        out_shape=([jax.ShapeDtypeStruct((t, w), BF16) for w in row_widths]
                   + [jax.ShapeDtypeStruct((hgt, t), BF16) for hgt in col_heights]),
        compiler_params=pltpu.CompilerParams(
            dimension_semantics=("parallel",), vmem_limit_bytes=VMEM_LIMIT),
        name="proj",
    )(pos3, x2d, invf, g_pre, w_in_p, g_qa, w_uq_t, g_kva, w_uk_p, w_uv_t, w_q_t, w_v_t)


def _ret_kernel(lg_ref, qt_ref, k_ref, vt_ref, g_ref, gn_ref, o_ref):
    for pair in range(PAIRS_RET):
        _ret_pair(pl.program_id(1) * PAIRS_RET + pair, slice(pair * LANES, (pair + 1) * LANES),
                  lg_ref, qt_ref, k_ref, vt_ref, g_ref, gn_ref, o_ref)


def _ret_pair(hp, span, lg_ref, qt_ref, k_ref, vt_ref, g_ref, gn_ref, o_ref):
    qt_ref, vt_ref = qt_ref.at[span, :], vt_ref.at[span, :]
    k_ref, g_ref, gn_ref, o_ref = k_ref.at[:, span], g_ref.at[:, span], gn_ref.at[:, span], o_ref.at[:, span]
    c = RET_CHUNK
    d = RET_HEAD_DIM
    n = k_ref.shape[0] // c
    lg0 = lg_ref[2 * hp]
    lg1 = lg_ref[2 * hp + 1]
    lane = lax.broadcasted_iota(jnp.int32, (1, LANES), 1)
    sub = lax.broadcasted_iota(jnp.int32, (LANES, 1), 0)
    lg_lane = jnp.where(lane < d, lg0, lg1)
    lg_sub = jnp.where(sub < d, lg0, lg1)
    key = lax.broadcasted_iota(jnp.int32, (c, c), 0)
    qry = lax.broadcasted_iota(jnp.int32, (c, c), 1)
    rel = (qry - key).astype(F32)
    causal = rel >= 0
    relp = jnp.maximum(rel, 0.0)
    intra = (jnp.where(causal, jnp.exp(lg0 * relp), 0.0), jnp.where(causal, jnp.exp(lg1 * relp), 0.0))
    tok_lane = lax.broadcasted_iota(jnp.int32, (1, c), 1).astype(F32)
    tok_sub = lax.broadcasted_iota(jnp.int32, (c, 1), 0).astype(F32)
    inner_t = jnp.exp(lg_sub * (tok_lane + 1.0))
    tail = jnp.exp(lg_lane * (c - 1.0 - tok_sub))
    decay = jnp.exp(lg_sub * float(c))
    same_head = (key < d) == (qry < d)
    zeros = jnp.zeros((d, c), BF16)
    heads = (0, 1)
    chunks = range(n)

    def tok(i):
        return slice(i * c, (i + 1) * c)

    q_t = [qt_ref[:, tok(i)] for i in chunks]
    q_h = [(jnp.concatenate([q_t[i][:d], zeros], axis=0), jnp.concatenate([zeros, q_t[i][d:]], axis=0))
           for i in chunks]
    s_t = [[(_dot(k_ref[tok(i), :], q_h[i][hh]) * intra[hh]).astype(BF16) for hh in heads] for i in chunks]
    w_t = [jnp.concatenate([_dot(vt_ref[hh * d:(hh + 1) * d, tok(i)], s_t[i][hh]) for hh in heads], axis=0)
           for i in chunks]
    upd = [jnp.where(same_head,
                     _dot(vt_ref[:, tok(i)], (k_ref[tok(i), :].astype(F32) * tail).astype(BF16)), 0.0)
           for i in chunks]
    states = [jnp.zeros((LANES, LANES), F32)]
    for i in range(n - 1):
        states.append(states[i] * decay + upd[i])
    gn = gn_ref[...]
    for i in chunks:
        o_t = w_t[i] if i == 0 else _dot(states[i].astype(BF16), q_t[i]) * inner_t + w_t[i]
        normed = []
        for hh in heads:
            o_h = o_t[hh * d:(hh + 1) * d]
            dlt = o_h - jnp.mean(o_h, axis=0, keepdims=True)
            var = jnp.mean(dlt * dlt, axis=0, keepdims=True)
            normed.append(dlt * lax.rsqrt(var + EPS))
        on = jnp.concatenate(normed, axis=0).T
        o_ref[tok(i), :] = (g_ref[tok(i), :].astype(F32) * (on * gn)).astype(BF16)


def _retention(log_gamma, q_rt, k_r, v_rt, g_r, g_gn, batch, seq):
    t = k_r.shape[0]
    width = PAIRS_RET * LANES
    row_blk = pl.BlockSpec((seq, width), lambda b, hp: (b, hp))
    col_blk = pl.BlockSpec((width, seq), lambda b, hp: (hp, b))
    return pl.pallas_call(
        _ret_kernel,
        grid=(batch, RET_WIDTH // width),
        in_specs=[
            pl.BlockSpec(memory_space=pltpu.SMEM),
            col_blk, row_blk, col_blk, row_blk,
            pl.BlockSpec((1, width), lambda b, hp: (0, hp)),
        ],
        out_specs=row_blk,
        out_shape=jax.ShapeDtypeStruct((t, RET_WIDTH), BF16),
        compiler_params=pltpu.CompilerParams(
            dimension_semantics=("parallel", "parallel"), vmem_limit_bytes=VMEM_LIMIT),
        name="retention",
    )(log_gamma, q_rt, k_r, v_rt, g_r, g_gn)


def _mla_kernel(qt_ref, k_ref, vt_ref, o_ref, m_ref, acc_ref):
    for pair in range(PAIRS_MLA):
        qk = slice(pair * 2 * LANES, (pair + 1) * 2 * LANES)
        vo = slice(pair * LANES, (pair + 1) * LANES)
        st = slice(pair * 2, (pair + 1) * 2)
        _mla_pair(qt_ref.at[qk, :], k_ref.at[:, qk], vt_ref.at[vo, :], o_ref.at[:, vo], m_ref.at[st], acc_ref.at[st])


def _mla_pair(qt_ref, k_ref, vt_ref, o_ref, m_ref, acc_ref):
    tile = TQ_MLA
    n_tiles = k_ref.shape[0] // tile
    key = lax.broadcasted_iota(jnp.int32, (tile, tile), 0)
    qry = lax.broadcasted_iota(jnp.int32, (tile, tile), 1)
    diag_ok = key <= qry
    heads = (0, 1)
    dv = MLA_V_DIM
    ones = jnp.ones((ONES_ROWS, tile), BF16)

    def scores(d, hh):
        return _dot(k_ref[d * tile:(d + 1) * tile, hh * LANES:(hh + 1) * LANES],
                    qt_ref[hh * LANES:(hh + 1) * LANES, d * tile:])

    def update(d, hh, s, cols):
        v_ext = jnp.concatenate([vt_ref[hh * dv:(hh + 1) * dv, d * tile:(d + 1) * tile], ones], axis=0)
        m_t = jnp.max(s, axis=0, keepdims=True)
        if d == 0:
            m_ref[hh, :, cols] = m_t
            acc_ref[hh, :, cols] = _dot(v_ext, jnp.exp2(s - m_t).astype(BF16))
        else:
            m = m_ref[hh, :, cols]
            m_new = jnp.maximum(m, m_t)
            m_ref[hh, :, cols] = m_new
            pv = _dot(v_ext, jnp.exp2(s - m_new).astype(BF16))
            acc_ref[hh, :, cols] = jnp.exp2(m - m_new) * acc_ref[hh, :, cols] + pv

    s_next = {hh: scores(0, hh) for hh in heads}
    for d in range(n_tiles):
        done = slice(d * tile, (d + 1) * tile)
        for hh in heads:
            s = s_next[hh]
            if d + 1 < n_tiles:
                s_next[hh] = scores(d + 1, hh)
            update(d, hh, jnp.where(diag_ok, s[:, :tile], -jnp.inf), done)
            for c0 in range(d + 1, n_tiles, COL_TILES_MLA):
                c1 = min(c0 + COL_TILES_MLA, n_tiles)
                update(d, hh, s[:, (c0 - d) * tile:(c1 - d) * tile], slice(c0 * tile, c1 * tile))
        o_t = jnp.concatenate([acc_ref[hh, :dv, done] / acc_ref[hh, dv:dv + 1, done] for hh in heads], axis=0)
        o_ref[done, :] = o_t.T.astype(BF16)


def _mla(q_t, k_m, v_t, batch, seq):
    t = k_m.shape[0]
    return pl.pallas_call(
        _mla_kernel,
        grid=(batch, MLA_WIDTH // (PAIRS_MLA * LANES)),
        in_specs=[
            pl.BlockSpec((PAIRS_MLA * 2 * LANES, seq), lambda b, hp: (hp, b)),
            pl.BlockSpec((seq, PAIRS_MLA * 2 * LANES), lambda b, hp: (b, hp)),
            pl.BlockSpec((PAIRS_MLA * LANES, seq), lambda b, hp: (hp, b)),
        ],
        out_specs=pl.BlockSpec((seq, PAIRS_MLA * LANES), lambda b, hp: (b, hp)),
        out_shape=jax.ShapeDtypeStruct((t, MLA_WIDTH), BF16),
        scratch_shapes=[
            pltpu.VMEM((PAIRS_MLA * 2, 1, seq), F32),
            pltpu.VMEM((PAIRS_MLA * 2, MLA_V_DIM + ONES_ROWS, seq), F32),
        ],
        compiler_params=pltpu.CompilerParams(
            dimension_semantics=("parallel", "parallel"), vmem_limit_bytes=VMEM_LIMIT),
        name="mla_attention",
    )(q_t, k_m, v_t)


def _causal_conv(u, cw, halo):
    row = lax.broadcasted_iota(jnp.int32, halo.shape, 0)

    def shifted(n):
        r = pltpu.roll(u, n, 0)
        first = jnp.where(row < n, pltpu.roll(halo, n, 0), r[:SUBLANES])
        return jnp.concatenate([first, r[SUBLANES:]], axis=0)

    return cw[0:1] * shifted(2) + cw[1:2] * shifted(1) + cw[2:3] * u + cw[3:4]


def _post_kernel(x_ref, yr_ref, ym_ref, p_ref, wo_ref, gpm_ref, gpf_ref, wup_ref, cw_ref, wdn_ref,
                 gff_ref, wple_ref, wgate_ref, bgate_ref, gple_ref, out_ref,
                 halo_ref, x1_ref, h2_ref, f_ref):
    tm = x_ref.shape[0]

    @pl.when(pl.program_id(1) == 0)
    def _():
        halo_ref[...] = jnp.zeros_like(halo_ref)

    n_blk = tm // ROWS_POST
    n_units = N_FF_CHUNKS * n_blk
    half = D_MODEL // 2

    def ff_cols(j, up=False):
        off = D_FF if up else 0
        return slice(off + FF_BOUNDS[j], off + FF_BOUNDS[j + 1])

    def rows_of(u):
        r0 = (u % n_blk) * ROWS_POST
        return slice(r0, r0 + ROWS_POST)

    mix = _dot(yr_ref[...], wo_ref[0:RET_WIDTH, :]) + _dot(ym_ref[...], wo_ref[RET_WIDTH:, :])
    x1 = x_ref[...] + _rms(mix, gpm_ref[...])
    x1_ref[...] = x1
    h2_ref[...] = _rms(x1, gpf_ref[...]).astype(BF16)
    e_in = _dot(p_ref[...].astype(BF16), wple_ref[...])

    def up_gate(u):
        return _dot(h2_ref[rows_of(u), :], wup_ref[:, ff_cols(u // n_blk)])

    def up_val(u):
        return _dot(h2_ref[rows_of(u), :], wup_ref[:, ff_cols(u // n_blk, up=True)])

    def geglu(u, ug, uu, prev, q):
        c, b = divmod(u, n_blk)
        qr = ROWS_POST // 4
        r0 = q * qr
        if r0 > 0:
            halo_g, halo_u = ug[r0 - SUBLANES:r0], uu[r0 - SUBLANES:r0]
        elif b > 0:
            halo_g, halo_u = prev[0][ROWS_POST - SUBLANES:], prev[1][ROWS_POST - SUBLANES:]
        else:
            halo_g, halo_u = halo_ref[:, ff_cols(c)], halo_ref[:, ff_cols(c, up=True)]
        gate = _causal_conv(ug[r0:r0 + qr], cw_ref[:, ff_cols(c)], halo_g)
        up = _causal_conv(uu[r0:r0 + qr], cw_ref[:, ff_cols(c, up=True)], halo_u)
        inner = gate * (GELU_C1 + GELU_C2 * (gate * gate))
        return ((gate * up) * (1.0 + jnp.tanh(inner))).astype(BF16)

    def down(u, act, n):
        c = u // n_blk
        cols = slice(n * half, (n + 1) * half)
        dn = _dot(act, wdn_ref[ff_cols(c), cols])
        if c == 0:
            f_ref[rows_of(u), cols] = dn
        else:
            f_ref[rows_of(u), cols] += dn

    cur = (up_gate(0), up_val(0))
    prev = None
    act_prev = None
    for u in range(n_units):
        more = u + 1 < n_units
        ug, uu = cur
        nxt_g = up_gate(u + 1) if more else None
        a0 = geglu(u, ug, uu, prev, 0)
        nxt_u = up_val(u + 1) if more else None
        a1 = geglu(u, ug, uu, prev, 1)
        if act_prev is not None:
            down(u - 1, act_prev, 0)
        a2 = geglu(u, ug, uu, prev, 2)
        if act_prev is not None:
            down(u - 1, act_prev, 1)
        a3 = geglu(u, ug, uu, prev, 3)
        act_prev = jnp.concatenate([a0, a1, a2, a3], axis=0)
        if u % n_blk == n_blk - 1:
            c = u // n_blk
            halo_ref[:, ff_cols(c)] = ug[ROWS_POST - SUBLANES:]
            halo_ref[:, ff_cols(c, up=True)] = uu[ROWS_POST - SUBLANES:]
        prev = cur
        cur = (nxt_g, nxt_u)
    down(n_units - 1, act_prev, 0)
    down(n_units - 1, act_prev, 1)

    for b in range(n_blk):
        rows = rows_of(b)
        x2 = x1_ref[rows, :] + _rms(f_ref[rows, :], gff_ref[...])
        ple_gate = jax.nn.sigmoid(_dot(x2.astype(BF16), wgate_ref[...]) + bgate_ref[...])
        out_ref[rows, :] = x2 + _rms(e_in[rows] * ple_gate, gple_ref[...])


def _post(x2d, y_ret, y_mla, p2d, w_o, g_pm, g_pf, w_up_c, cw_c, w_dn_c, g_ff, w_ple, w_gate, b_gate,
          g_ple, batch, seq):
    t = x2d.shape[0]
    tm = TM_POST
    ns = seq // tm
    row = lambda b, s: (b * ns + s, 0)
    consts = (w_o, g_pm, g_pf, w_up_c, cw_c, w_dn_c, g_ff, w_ple, w_gate, b_gate, g_ple)
    return pl.pallas_call(
        _post_kernel,
        grid=(batch, ns),
        in_specs=[
            pl.BlockSpec((tm, D_MODEL), row),
            pl.BlockSpec((tm, RET_WIDTH), row),
            pl.BlockSpec((tm, MLA_WIDTH), row),
            pl.BlockSpec((tm, PLE_DIM), row),
        ] + [_const_spec(a.shape) for a in consts],
        out_specs=pl.BlockSpec((tm, D_MODEL), row),
        out_shape=jax.ShapeDtypeStruct((t, D_MODEL), F32),
        scratch_shapes=[
            pltpu.VMEM((SUBLANES, 2 * D_FF), F32),
            pltpu.VMEM((tm, D_MODEL), F32),
            pltpu.VMEM((tm, D_MODEL), BF16),
            pltpu.VMEM((tm, D_MODEL), F32),
        ],
        compiler_params=pltpu.CompilerParams(
            dimension_semantics=("arbitrary", "arbitrary"), vmem_limit_bytes=VMEM_LIMIT),
        name="post",
    )(x2d, y_ret, y_mla, p2d, *consts)


def _layer(x2d, p2d, pos3, batch, seq, w_in, w_uq, w_ukv, w_o, g_pre_mix, g_post_mix, g_q_a, g_kv_a,
           g_ret_gn, w_up, conv_w, conv_b, w_down, g_pre_ffn, g_post_ffn, w_ple, w_ple_gate,
           b_ple_gate, g_post_ple):
    row = lambda v: v.reshape(1, -1)

    kpe_blk = jnp.pad(w_in[:, IN_USED:], ((0, 0), (MLA_NOPE_DIM, LANES - MLA_QK_DIM)))
    w_in_p = jnp.concatenate([w_in[:, RET_WIDTH:2 * RET_WIDTH], w_in[:, 3 * RET_WIDTH:IN_USED], kpe_blk],
                             axis=1).astype(BF16)
    w_q_t = w_in[:, :RET_WIDTH].T.astype(BF16)
    w_v_t = w_in[:, 2 * RET_WIDTH:3 * RET_WIDTH].T.astype(BF16)
    w_uq_t = jnp.pad(w_uq.reshape(MLA_Q_RANK, MLA_HEADS, MLA_QK_DIM),
                     ((0, 0), (0, 0), (0, LANES - MLA_QK_DIM))).reshape(MLA_Q_RANK, MLA_PAD).T.astype(BF16)
    w_ukv3 = w_ukv.reshape(MLA_KV_RANK, MLA_HEADS, MLA_NOPE_DIM + MLA_V_DIM)
    w_uk_p = jnp.pad(w_ukv3[..., :MLA_NOPE_DIM],
                     ((0, 0), (0, 0), (0, LANES - MLA_NOPE_DIM))).reshape(MLA_KV_RANK, MLA_PAD).astype(BF16)
    w_uv_t = w_ukv3[..., MLA_NOPE_DIM:].reshape(MLA_KV_RANK, MLA_WIDTH).T.astype(BF16)
    w_up_c = w_up.astype(BF16)
    w_dn_c = (0.5 * w_down).astype(BF16)
    cw4 = jnp.concatenate([conv_w, conv_b[None, :]], axis=0)
    cw_c = jnp.pad(cw4, ((0, SUBLANES - 4), (0, 0)))

    invf_r = ROPE_BASE ** (-jnp.arange(0, RET_HEAD_DIM, 2, dtype=F32) / RET_HEAD_DIM)
    invf_m = ROPE_BASE ** (-jnp.arange(0, MLA_ROPE_DIM, 2, dtype=F32) / MLA_ROPE_DIM)
    invf = jnp.concatenate([invf_r, invf_m]).reshape(-1, 1)
    log_gamma = jnp.log1p(-jnp.exp2(-5.0 - jnp.arange(RET_HEADS, dtype=F32)))

    k_r, g_r, k_m, q_rt, v_rt, q_t, v_t = _proj(
        pos3, x2d, invf, row(g_pre_mix), w_in_p, row(g_q_a), w_uq_t, row(g_kv_a), w_uk_p, w_uv_t, w_q_t, w_v_t)
    y_ret = _retention(log_gamma, q_rt, k_r, v_rt, g_r, row(g_ret_gn), batch, seq)
    y_mla = _mla(q_t, k_m, v_t, batch, seq)
    return _post(x2d, y_ret, y_mla, p2d, w_o.astype(BF16), row(g_post_mix), row(g_pre_ffn), w_up_c, cw_c,
                 w_dn_c, row(g_post_ffn), w_ple.astype(BF16), w_ple_gate.astype(BF16), row(b_ple_gate),
                 row(g_post_ple), batch, seq)


def kernel(x, p, positions, w_in, w_uq, w_ukv, w_o, g_pre_mix, g_post_mix, g_q_a, g_kv_a, g_ret_gn,
           w_up, conv_w, conv_b, w_down, g_pre_ffn, g_post_ffn, w_ple, w_ple_gate, b_ple_gate,
           g_post_ple):
    batch, seq, _ = x.shape
    depth = w_in.shape[0]
    assert seq % TM_PROJ == 0 and seq % TM_POST == 0 and seq % TQ_MLA == 0 and seq % RET_CHUNK == 0
    t = batch * seq
    x2d = x.reshape(t, D_MODEL)
    pos3 = positions.reshape(t // TM_PROJ, 1, TM_PROJ)
    for i in range(depth):
        x2d = _layer(x2d, p[i].reshape(t, PLE_DIM), pos3, batch, seq, w_in[i], w_uq[i], w_ukv[i], w_o[i],
                     g_pre_mix[i], g_post_mix[i], g_q_a[i], g_kv_a[i], g_ret_gn[i], w_up[i], conv_w[i],
                     conv_b[i], w_down[i], g_pre_ffn[i], g_post_ffn[i], w_ple[i], w_ple_gate[i],
                     b_ple_gate[i], g_post_ple[i])
    return x2d.reshape(batch, seq, D_MODEL)
```
